```python
import jax, jax.numpy as jnp
from jax import lax
import numpy as np

D_MODEL = 2048
BATCH = 1
SEQ = 16384
DEPTH = 4
DEC_BATCH = 1
DEC_SEQ = 8192
PAST_LEN = 128

D_R = 2048
H_R = 16
BR = D_R // H_R
CONV_W = 4
CONV_LEFT = 2
RG_C = 8.0
D_H = 2048
H_H = 16
DK = D_H // H_H
DV = D_H // H_H
CHUNK = 64
IN_SPLITS = (D_R, D_R, D_H, D_H, D_H, D_H, D_H, D_MODEL, D_MODEL)
IN_COLS = 2 * D_R + 5 * D_H + 2 * D_MODEL
EPS = 1e-6

kernel_name = "hawk_hgrn2_bidir_gated_parallel_encoder"


def rmsnorm(x, gain, eps=EPS):
    xf = x.astype(jnp.float32)
    y = xf * lax.rsqrt(jnp.mean(xf * xf, axis=-1, keepdims=True) + eps)
    return (y * gain.astype(jnp.float32)).astype(x.dtype)


def centred_depthwise_conv(x, w, b):
    s = x.shape[1]
    xp = jnp.pad(x, ((0, 0), (CONV_LEFT, CONV_W - 1 - CONV_LEFT), (0, 0)))
    out = b + xp[:, 0:s] * w[0]
    for j in range(1, CONV_W):
        out = out + xp[:, j:j + s] * w[j]
    return out


def _lin_combine(left, right):
    a1, b1 = left
    a2, b2 = right
    return a1 * a2, a2 * b1 + b2


def rglru_direction(xf, wa, ba, wx, bx, lam, reverse):
    bsz, s, _ = xf.shape
    xb = xf.reshape(bsz, s, H_R, BR)
    r = jax.nn.sigmoid(jnp.einsum('bshi,hij->bshj', xb, wa.astype(jnp.float32)).reshape(bsz, s, D_R) + ba.astype(jnp.float32))
    i = jax.nn.sigmoid(jnp.einsum('bshi,hij->bshj', xb, wx.astype(jnp.float32)).reshape(bsz, s, D_R) + bx.astype(jnp.float32))
    log_a = -RG_C * r * jax.nn.softplus(-lam.astype(jnp.float32))
    a = jnp.exp(log_a)
    bval = jnp.sqrt(-jnp.expm1(2.0 * log_a)) * (i * xf)
    _, h = lax.associative_scan(_lin_combine, (a, bval), axis=1, reverse=reverse)
    return h


def hgrn2_forward_scan(q, k, v, logf):
    bsz, s = q.shape[0], q.shape[1]
    nc = s // CHUNK

    def to_chunks(t):
        return t.reshape(bsz, nc, CHUNK, H_H, t.shape[-1]).transpose(1, 0, 3, 2, 4)

    mask = jnp.tril(jnp.ones((CHUNK, CHUNK), dtype=bool))[:, :, None]

    def body(state, inp):
        qc, kc, vc, gc = inp
        bcum = jnp.cumsum(gc, axis=-2)
        o_inter = jnp.einsum('bhtk,bhkv->bhtv', qc * jnp.exp(bcum), state)
        diff = bcum[:, :, :, None, :] - bcum[:, :, None, :, :]
        decay = jnp.exp(jnp.where(mask, diff, -jnp.inf))
        scores = jnp.einsum('bhtk,bhsk,bhtsk->bhts', qc, kc, decay)
        o_intra = jnp.einsum('bhts,bhsv->bhtv', scores, vc)
        blast = bcum[:, :, -1:, :]
        kd = kc * jnp.exp(blast - bcum)
        new_state = jnp.exp(blast[:, :, 0, :])[..., None] * state + jnp.einsum('bhsk,bhsv->bhkv', kd, vc)
        return new_state, o_inter + o_intra

    init = jnp.zeros((bsz, H_H, DK, DV), jnp.float32)
    _, o = lax.scan(body, init, (to_chunks(q), to_chunks(k), to_chunks(v), to_chunks(logf)))
    return o.transpose(1, 0, 3, 2, 4).reshape(bsz, s, H_H, DV)


def hgrn2_gates(z, lb):
    logf = jnp.logaddexp(jnp.log(lb), jnp.log1p(-lb) + jax.nn.log_sigmoid(z))
    k = (1.0 - lb) * jax.nn.sigmoid(-z)
    return logf, k


def trunk(x, norm_gain, w_in, conv_w, conv_b, rg_wa, rg_ba, rg_wx, rg_bx, rg_lambda,
          hg_lower, hg_norm_gain, w_down_r, w_down_h, w_out, final_gain):
    bsz, s, _ = x.shape
    lb_all = jnp.cumsum(jax.nn.softmax(hg_lower.astype(jnp.float32), axis=1), axis=1)
    lb_all = lb_all - lb_all[:, :1]
    split_idx = list(np.cumsum(IN_SPLITS)[:-1])
    for l in range(DEPTH):
        h = rmsnorm(x, norm_gain[l])
        proj = h @ w_in[l]
        xr, gr, qh, zf, zb, vh, gh, mr, mh = jnp.split(proj, split_idx, axis=-1)

        xc = centred_depthwise_conv(xr, conv_w[l], conv_b[l]).astype(jnp.float32)
        hr = (rglru_direction(xc, rg_wa[l, 0], rg_ba[l, 0], rg_wx[l, 0], rg_bx[l, 0], rg_lambda[l, 0], False)
              + rglru_direction(xc, rg_wa[l, 1], rg_ba[l, 1], rg_wx[l, 1], rg_bx[l, 1], rg_lambda[l, 1], True))
        yr = (hr * jax.nn.silu(gr.astype(jnp.float32))).astype(x.dtype) @ w_down_r[l]

        q4 = jax.nn.silu(qh.astype(jnp.float32)).reshape(bsz, s, H_H, DK)
        v4 = vh.astype(jnp.float32).reshape(bsz, s, H_H, DV)
        logf_f, k_f = hgrn2_gates(zf.astype(jnp.float32), lb_all[0, l])
        logf_b, k_b = hgrn2_gates(zb.astype(jnp.float32), lb_all[1, l])
        o_f = hgrn2_forward_scan(q4, k_f.reshape(bsz, s, H_H, DK), v4, logf_f.reshape(bsz, s, H_H, DK))
        o_b = jnp.flip(hgrn2_forward_scan(jnp.flip(q4, 1), jnp.flip(k_b.reshape(bsz, s, H_H, DK), 1),
                                          jnp.flip(v4, 1), jnp.flip(logf_b.reshape(bsz, s, H_H, DK), 1)), 1)
        o = rmsnorm(o_f + o_b, hg_norm_gain[l].reshape(H_H, DV)).reshape(bsz, s, D_H)
        yh = (o * jax.nn.silu(gh.astype(jnp.float32))).astype(x.dtype) @ w_down_h[l]

        merged = jax.nn.sigmoid(mr) * yr + jax.nn.sigmoid(mh) * yh
        x = x + merged @ w_out[l]
    return rmsnorm(x, final_gain)


def setup_inputs(seed: int = 0) -> dict:
    key = jax.random.key(seed)
    ks = jax.random.split(key, 18)
    f32 = jnp.float32
    x_prompt = jax.random.normal(ks[0], (BATCH, SEQ, D_MODEL), f32)
    x_sample = jax.random.normal(ks[1], (DEC_BATCH, DEC_SEQ, D_MODEL), f32)
    norm_gain = 1.0 + 0.02 * jax.random.normal(ks[2], (DEPTH, D_MODEL), f32)
    w_in = jax.random.normal(ks[3], (DEPTH, D_MODEL, IN_COLS), f32) * D_MODEL ** -0.5
    conv_w = jax.random.normal(ks[4], (DEPTH, CONV_W, D_R), f32) * CONV_W ** -0.5
    conv_b = 0.02 * jax.random.normal(ks[5], (DEPTH, D_R), f32)
    rg_wa = jax.random.normal(ks[6], (DEPTH, 2, H_R, BR, BR), f32) * BR ** -0.5
    rg_ba = 0.02 * jax.random.normal(ks[7], (DEPTH, 2, D_R), f32)
    rg_wx = jax.random.normal(ks[8], (DEPTH, 2, H_R, BR, BR), f32) * BR ** -0.5
    rg_bx = 0.02 * jax.random.normal(ks[9], (DEPTH, 2, D_R), f32)
    a0 = jax.random.uniform(ks[10], (DEPTH, 2, D_R), f32, 0.9, 0.999)
    p = a0 ** (1.0 / RG_C)
    rg_lambda = jnp.log(p) - jnp.log1p(-p)
    hg_lower = 0.1 * jax.random.normal(ks[11], (2, DEPTH, D_H), f32)
    hg_norm_gain = 1.0 + 0.02 * jax.random.normal(ks[12], (DEPTH, D_H), f32)
    w_down_r = jax.random.normal(ks[13], (DEPTH, D_R, D_MODEL), f32) * D_R ** -0.5
    w_down_h = jax.random.normal(ks[14], (DEPTH, D_H, D_MODEL), f32) * D_H ** -0.5
    w_out = jax.random.normal(ks[15], (DEPTH, D_MODEL, D_MODEL), f32) * (D_MODEL * 2 * DEPTH) ** -0.5
    final_gain = 1.0 + 0.02 * jax.random.normal(ks[16], (D_MODEL,), f32)
    return {"x_prompt": x_prompt, "x_sample": x_sample, "norm_gain": norm_gain, "w_in": w_in,
            "conv_w": conv_w, "conv_b": conv_b, "rg_wa": rg_wa, "rg_ba": rg_ba, "rg_wx": rg_wx,
            "rg_bx": rg_bx, "rg_lambda": rg_lambda, "hg_lower": hg_lower, "hg_norm_gain": hg_norm_gain,
            "w_down_r": w_down_r, "w_down_h": w_down_h, "w_out": w_out, "final_gain": final_gain}


def reference(x_prompt, x_sample, norm_gain, w_in, conv_w, conv_b, rg_wa, rg_ba, rg_wx, rg_bx,
              rg_lambda, hg_lower, hg_norm_gain, w_down_r, w_down_h, w_out, final_gain):
    y_prompt = trunk(x_prompt, norm_gain, w_in, conv_w, conv_b, rg_wa, rg_ba, rg_wx, rg_bx, rg_lambda,
                     hg_lower, hg_norm_gain, w_down_r, w_down_h, w_out, final_gain)
    y_sample = trunk(x_sample, norm_gain, w_in, conv_w, conv_b, rg_wa, rg_ba, rg_wx, rg_bx, rg_lambda,
                     hg_lower, hg_norm_gain, w_down_r, w_down_h, w_out, final_gain)
    return (y_prompt, y_sample)
```

```python
import functools

import jax
import jax.numpy as jnp
from jax import lax
from jax.experimental import pallas as pl
from jax.experimental.pallas import tpu as pltpu

F32 = jnp.float32
BF16 = jnp.bfloat16

D_MODEL = 2048
HEAD = 128
N_SPLITS = 9
IN_COLS = N_SPLITS * D_MODEL
COL_XR, COL_GR, COL_Q, COL_ZF, COL_ZB, COL_V, COL_GH, COL_MR, COL_MH = range(N_SPLITS)
EPS = 1e-6
RG_C = 8.0
CHUNK = 64
HALO = 16
VMEM_LIMIT = 56 * 1024 * 1024


def _sigmoid(x):
    return 1.0 / (1.0 + jnp.exp(-x))


def _softplus(x):
    return jnp.maximum(x, 0.0) + jnp.log1p(jnp.exp(-jnp.abs(x)))


def _largest_tile(n, cap):
    t = cap
    while n % t:
        t //= 2
    return t


def _inproj_body(x_ref, g_ref, w_ref, o_ref, h_ref, *, tm):
    @pl.when(pl.program_id(1) == 0)
    def _():
        rows = 128
        def norm(i, c):
            sl = pl.ds(pl.multiple_of(i * rows, rows), rows)
            x = x_ref[sl, :]
            ms = jnp.mean(x * x, axis=-1, keepdims=True)
            h_ref[sl, :] = (x * lax.rsqrt(ms + EPS) * g_ref[...]).astype(BF16)
            return c
        lax.fori_loop(0, tm // rows, norm, 0)

    o_ref[...] = jnp.dot(h_ref[...], w_ref[...], preferred_element_type=F32).astype(o_ref.dtype)


def _inproj(x, gain, w_in, layer, tm, tn):
    t = x.shape[0]
    return pl.pallas_call(
        functools.partial(_inproj_body, tm=tm),
        grid=(t // tm, IN_COLS // tn),
        in_specs=[
            pl.BlockSpec((tm, D_MODEL), lambda i, j: (i, 0)),
            pl.BlockSpec((None, 1, D_MODEL), lambda i, j: (layer, 0, 0)),
            pl.BlockSpec((None, D_MODEL, tn), lambda i, j: (layer, 0, j)),
        ],
        out_specs=pl.BlockSpec((tm, tn), lambda i, j: (i, j)),
        out_shape=jax.ShapeDtypeStruct((t, IN_COLS), BF16),
        scratch_shapes=[pltpu.VMEM((tm, D_MODEL), BF16)],
        compiler_params=pltpu.CompilerParams(
            dimension_semantics=("arbitrary", "arbitrary"), vmem_limit_bytes=VMEM_LIMIT),
        name="inproj",
    )(x, gain, w_in)


def _rglru_body(*refs, reverse, n_a, n_t, tb, cb, rc):
    if reverse:
        (x_ref, xp_ref, xn_ref, gr_ref, hf_ref, cw_ref, cbias_ref, w_ref, ba_ref, bx_ref,
         lam_ref, o_ref, xs_ref, a_s, h_s, carry_ref) = refs
    else:
        (x_ref, xp_ref, xn_ref, cw_ref, cbias_ref, w_ref, ba_ref, bx_ref,
         lam_ref, o_ref, xs_ref, a_s, h_s, carry_ref) = refs
    nh = cb // HEAD
    nchunk = tb // rc
    nv = rc // 8
    t = pl.program_id(1)
    tix = (n_t - 1 - t) if reverse else t
    seq_first = jnp.logical_or(tix == 0, tix == n_a)
    seq_last = jnp.logical_or(tix == n_a - 1, tix == n_t - 1)
    start = seq_last if reverse else seq_first

    @pl.when(start)
    def _():
        carry_ref[...] = jnp.zeros_like(carry_ref)

    pm = jnp.where(seq_first, 0.0, 1.0).astype(F32)
    nm = jnp.where(seq_last, 0.0, 1.0).astype(F32)
    xs_ref[0:8, :] = xp_ref[...].astype(F32)[HALO - 8:HALO, :] * pm
    xs_ref[8:8 + tb, :] = x_ref[...].astype(F32)
    xs_ref[8 + tb:16 + tb, :] = xn_ref[...].astype(F32)[0:8, :] * nm

    neg_c_sp = -RG_C * _softplus(-lam_ref[...])
    sub = lax.broadcasted_iota(jnp.int32, (rc, HEAD), 0) & 7

    def chunk(k, c):
        kk = (nchunk - 1 - k) if reverse else k
        r0 = pl.multiple_of(kk * rc, rc)
        for h in range(nh):
            hs = slice(h * HEAD, (h + 1) * HEAD)
            xw = xs_ref[pl.ds(r0, rc + 16), hs]
            xc = (cbias_ref[:, hs]
                  + xw[6:6 + rc] * cw_ref[0:1, hs]
                  + xw[7:7 + rc] * cw_ref[1:2, hs]
                  + xw[8:8 + rc] * cw_ref[2:3, hs]
                  + xw[9:9 + rc] * cw_ref[3:4, hs])
            g = jnp.dot(xc.astype(BF16), w_ref[h], preferred_element_type=F32)
            r = _sigmoid(g[:, :HEAD] + ba_ref[:, hs])
            i = _sigmoid(g[:, HEAD:] + bx_ref[:, hs])
            log_a = r * neg_c_sp[:, hs]
            a = jnp.exp(log_a)
            bv = jnp.sqrt(1.0 - a * a) * (i * xc)
            for d in (1, 2, 4):
                if reverse:
                    a_sh = pltpu.roll(a, rc - d, axis=0)
                    b_sh = pltpu.roll(bv, rc - d, axis=0)
                    valid = sub < 8 - d
                else:
                    a_sh = pltpu.roll(a, d, axis=0)
                    b_sh = pltpu.roll(bv, d, axis=0)
                    valid = sub >= d
                bv = jnp.where(valid, a * b_sh + bv, bv)
                a = jnp.where(valid, a * a_sh, a)
            a_s[h] = a
            h_s[h] = bv

        def step(i, cs):
            ii = (nv - 1 - i) if reverse else i
            sl = pl.ds(pl.multiple_of(ii * 8, 8), 8)
            out = []
            for h in range(nh):
                hv = h_s[h, sl, :] + a_s[h, sl, :] * cs[h]
                h_s[h, sl, :] = hv
                out.append(hv[0:1, :] if reverse else hv[7:8, :])
            return tuple(out)

        cs = tuple(carry_ref[:, h * HEAD:(h + 1) * HEAD] for h in range(nh))
        cs = lax.fori_loop(0, nv, step, cs, unroll=True)
        for h in range(nh):
            hs = slice(h * HEAD, (h + 1) * HEAD)
            carry_ref[:, hs] = cs[h]
            rows = pl.ds(r0, rc)
            if reverse:
                gate = gr_ref[rows, hs].astype(F32)
                o_ref[rows, hs] = ((hf_ref[rows, hs] + h_s[h]) * (gate * _sigmoid(gate))).astype(o_ref.dtype)
            else:
                o_ref[rows, hs] = h_s[h]
        return c

    lax.fori_loop(0, nchunk, chunk, 0)


def _rglru(proj, hf, conv_w, conv_b, w_cat, ba, bx, lam, layer, n_a, reverse, tb, cb, rc):
    t = proj.shape[0]
    n_t = t // tb
    ncg = D_MODEL // cb
    hb = tb // HALO
    nhalo = t // HALO
    d = 1 if reverse else 0

    def tmap(ti):
        return (n_t - 1 - ti) if reverse else ti

    blk = lambda col: pl.BlockSpec((tb, cb), lambda c, ti: (tmap(ti), col * ncg + c))
    vec = lambda: pl.BlockSpec((None, 1, cb), lambda c, ti: (layer, 0, c))
    dvec = lambda: pl.BlockSpec((None, None, 1, cb), lambda c, ti: (layer, d, 0, c))
    in_specs = [
        blk(COL_XR),
        pl.BlockSpec((HALO, cb), lambda c, ti: (jnp.maximum(tmap(ti) * hb - 1, 0), c)),
        pl.BlockSpec((HALO, cb), lambda c, ti: (jnp.minimum((tmap(ti) + 1) * hb, nhalo - 1), c)),
    ]
    args = [proj, proj, proj]
    if reverse:
        in_specs += [blk(COL_GR), pl.BlockSpec((tb, cb), lambda c, ti: (tmap(ti), c))]
        args += [proj, hf]
    in_specs += [
        pl.BlockSpec((None, 4, cb), lambda c, ti: (layer, 0, c)),
        vec(),
        pl.BlockSpec((None, None, cb // HEAD, HEAD, 2 * HEAD), lambda c, ti: (layer, d, c, 0, 0)),
        dvec(), dvec(), dvec(),
    ]
    args += [conv_w, conv_b, w_cat, ba, bx, lam]
    nh = cb // HEAD
    return pl.pallas_call(
        functools.partial(_rglru_body, reverse=reverse, n_a=n_a, n_t=n_t, tb=tb, cb=cb, rc=rc),
        grid=(ncg, n_t),
        in_specs=in_specs,
        out_specs=pl.BlockSpec((tb, cb), lambda c, ti: (tmap(ti), c)),
        out_shape=jax.ShapeDtypeStruct((t, D_MODEL), BF16 if reverse else F32),
        scratch_shapes=[
            pltpu.VMEM((tb + 16, cb), F32),
            pltpu.VMEM((nh, rc, HEAD), F32),
            pltpu.VMEM((nh, rc, HEAD), F32),
            pltpu.VMEM((1, cb), F32),
        ],
        compiler_params=pltpu.CompilerParams(
            dimension_semantics=("arbitrary", "arbitrary"), vmem_limit_bytes=VMEM_LIMIT),
        name="rglru_bwd" if reverse else "rglru_fwd",
    )(*args)


def _level_ref(b, m, reverse):
    off = m if reverse else m - 1
    if m >= 8:
        parts = [jnp.broadcast_to(b[s + off:s + off + 1, :], (2 * m, HEAD))
                 for s in range(0, CHUNK, 2 * m)]
        return parts[0] if len(parts) == 1 else jnp.concatenate(parts, axis=0)
    sub = lax.broadcasted_iota(jnp.int32, (CHUNK, HEAD), 0) & 7
    b3 = b.reshape(CHUNK // 8, 8, HEAD)

    def row(j):
        return jnp.broadcast_to(b3[:, j:j + 1, :], (CHUNK // 8, 8, HEAD)).reshape(CHUNK, HEAD)

    out = row(off)
    for s in range(2 * m, 8, 2 * m):
        out = jnp.where(sub >= s, row(s + off), out)
    return out


def _hgrn_body(*refs, reverse, layer, n_a, n_t, tb, cb):
    if reverse:
        (q_ref, z_ref, v_ref, g_ref, of_ref, lower_ref, gain_ref, o_ref, st_ref) = refs
    else:
        (q_ref, z_ref, v_ref, lower_ref, o_ref, st_ref) = refs
    nh = cb // HEAD
    nchunk = tb // CHUNK
    t = pl.program_id(1)
    tix = (n_t - 1 - t) if reverse else t
    if reverse:
        start = jnp.logical_or(tix == n_a - 1, tix == n_t - 1)
    else:
        start = jnp.logical_or(tix == 0, tix == n_a)

    @pl.when(start)
    def _():
        st_ref[...] = jnp.zeros_like(st_ref)

    low = lower_ref[...]
    e = jnp.exp(low - jnp.max(low, axis=0, keepdims=True))
    den = jnp.sum(e, axis=0, keepdims=True)
    if layer == 0:
        lb = jnp.zeros_like(den)
    else:
        lb = jnp.sum(e[1:layer + 1, :], axis=0, keepdims=True) / den
    log_lb = jnp.log(lb)
    log1m_lb = jnp.log1p(-lb)

    ti = lax.broadcasted_iota(jnp.int32, (CHUNK, CHUNK), 0)
    si = lax.broadcasted_iota(jnp.int32, (CHUNK, CHUNK), 1)
    txs = ti ^ si
    tri = (ti <= si if reverse else ti >= si).astype(F32)
    rowi = lax.broadcasted_iota(jnp.int32, (CHUNK, HEAD), 0)

    def chunk(k, c):
        kk = (nchunk - 1 - k) if reverse else k
        rows = pl.ds(pl.multiple_of(kk * CHUNK, CHUNK), CHUNK)
        for h in range(nh):
            hs = slice(h * HEAD, (h + 1) * HEAD)
            z = z_ref[rows, hs].astype(F32)
            qx = q_ref[rows, hs].astype(F32)
            q = qx * _sigmoid(qx)
            v = v_ref[rows, hs]
            ez = jnp.exp(-jnp.abs(z))
            log_sig = jnp.minimum(z, 0.0) - jnp.log1p(ez)
            u = log1m_lb[:, hs] + log_sig
            la = log_lb[:, hs]
            logf = jnp.maximum(la, u) + jnp.log1p(jnp.exp(-jnp.abs(la - u)))
            key = (1.0 - lb[:, hs]) * (jnp.where(z >= 0.0, ez, 1.0) / (1.0 + ez))
            b = jnp.dot(tri, logf, precision=lax.Precision.HIGHEST, preferred_element_type=F32)

            scores = jnp.dot(q.astype(BF16), key.astype(BF16).T, preferred_element_type=F32)
            for m in (1, 2, 4, 8, 16, 32):
                e_l = jnp.exp(-jnp.abs(b - _level_ref(b, m, reverse)))
                upper = (rowi & m) != 0
                q_side = jnp.logical_not(upper) if reverse else upper
                q_l = jnp.where(q_side, q * e_l, 0.0).astype(BF16)
                k_l = jnp.where(q_side, 0.0, key * e_l).astype(BF16)
                p_l = lax.dot_general(q_l, k_l, (((1,), (1,)), ((), ())), preferred_element_type=F32)
                scores = jnp.where(txs >= m, p_l, scores)

            st = st_ref[h]
            q_in = (q * jnp.exp(b)).astype(BF16)
            o = lax.dot_general(q_in, st.astype(BF16), (((1,), (1,)), ((), ())), preferred_element_type=F32)
            o = o + jnp.dot(scores.astype(BF16), v, preferred_element_type=F32)

            b_end = b[0:1, :] if reverse else b[CHUNK - 1:CHUNK, :]
            kd = (key * jnp.exp(b_end - b)).astype(BF16)
            st_ref[h] = st * jnp.exp(b_end) + lax.dot_general(
                v, kd, (((0,), (0,)), ((), ())), preferred_element_type=F32)

            if reverse:
                ot = of_ref[rows, hs] + o
                ms = jnp.mean(ot * ot, axis=-1, keepdims=True)
                y = ot * lax.rsqrt(ms + EPS) * gain_ref[:, hs]
                gate = g_ref[rows, hs].astype(F32)
                o_ref[rows, hs] = (y * (gate * _sigmoid(gate))).astype(o_ref.dtype)
            else:
                o_ref[rows, hs] = o
        return c

    lax.fori_loop(0, nchunk, chunk, 0)


def _hgrn(proj, of, hg_lower, hg_gain, layer, n_a, reverse, tb, cb):
    t = proj.shape[0]
    n_t = t // tb
    ncg = D_MODEL // cb
    depth = hg_lower.shape[1]
    d = 1 if reverse else 0

    def tmap(ti):
        return (n_t - 1 - ti) if reverse else ti

    blk = lambda col: pl.BlockSpec((tb, cb), lambda c, ti: (tmap(ti), col * ncg + c))
    in_specs = [blk(COL_Q), blk(COL_ZB if reverse else COL_ZF), blk(COL_V)]
    args = [proj, proj, proj]
    if reverse:
        in_specs += [blk(COL_GH), pl.BlockSpec((tb, cb), lambda c, ti: (tmap(ti), c))]
        args += [proj, of]
    in_specs.append(pl.BlockSpec((None, depth, cb), lambda c, ti: (d, 0, c)))
    args.append(hg_lower)
    if reverse:
        in_specs.append(pl.BlockSpec((None, 1, cb), lambda c, ti: (layer, 0, c)))
        args.append(hg_gain)
    return pl.pallas_call(
        functools.partial(_hgrn_body, reverse=reverse, layer=layer, n_a=n_a, n_t=n_t, tb=tb, cb=cb),
        grid=(ncg, n_t),
        in_specs=in_specs,
        out_specs=pl.BlockSpec((tb, cb), lambda c, ti: (tmap(ti), c)),
        out_shape=jax.ShapeDtypeStruct((t, D_MODEL), BF16 if reverse else F32),
        scratch_shapes=[pltpu.VMEM((cb // HEAD, HEAD, HEAD), F32)],
        compiler_params=pltpu.CompilerParams(
            dimension_semantics=("arbitrary", "arbitrary"), vmem_limit_bytes=VMEM_LIMIT),
        name="hgrn_bwd" if reverse else "hgrn_fwd",
    )(*args)


def _merge_body(x_ref, ar_ref, ah_ref, mr_ref, mh_ref, wr_ref, wh_ref, wo_ref, fg_ref, o_ref, *, final):
    yr = jnp.dot(ar_ref[...], wr_ref[...], preferred_element_type=F32)
    yh = jnp.dot(ah_ref[...], wh_ref[...], preferred_element_type=F32)
    merged = _sigmoid(mr_ref[...].astype(F32)) * yr + _sigmoid(mh_ref[...].astype(F32)) * yh
    x = x_ref[...] + jnp.dot(merged.astype(BF16), wo_ref[...], preferred_element_type=F32)
    if final:
        ms = jnp.mean(x * x, axis=-1, keepdims=True)
        x = x * lax.rsqrt(ms + EPS) * fg_ref[...]
    o_ref[...] = x


def _merge(x, a_r, a_h, proj, w_dr, w_dh, w_o, final_gain, layer, final, tm):
    t = x.shape[0]
    row = lambda col: pl.BlockSpec((tm, D_MODEL), lambda i: (i, col))
    wgt = lambda: pl.BlockSpec((None, D_MODEL, D_MODEL), lambda i: (layer, 0, 0),
                               pipeline_mode=pl.Buffered(1))
    return pl.pallas_call(
        functools.partial(_merge_body, final=final),
        grid=(t // tm,),
        in_specs=[row(0), row(0), row(0), row(COL_MR), row(COL_MH), wgt(), wgt(), wgt(),
                  pl.BlockSpec((1, D_MODEL), lambda i: (0, 0))],
        out_specs=row(0),
        out_shape=jax.ShapeDtypeStruct((t, D_MODEL), F32),
        compiler_params=pltpu.CompilerParams(
            dimension_semantics=("arbitrary",), vmem_limit_bytes=VMEM_LIMIT),
        name="merge",
    )(x, a_r, a_h, proj, proj, w_dr, w_dh, w_o, final_gain)


def kernel(x_prompt, x_sample, norm_gain, w_in, conv_w, conv_b, rg_wa, rg_ba, rg_wx, rg_bx, rg_lambda,
           hg_lower, hg_norm_gain, w_down_r, w_down_h, w_out, final_gain):
    depth = w_in.shape[0]
    s_a = x_prompt.shape[0] * x_prompt.shape[1]
    s_b = x_sample.shape[0] * x_sample.shape[1]
    assert x_prompt.shape[0] == 1 and x_sample.shape[0] == 1
    x = jnp.concatenate([x_prompt.reshape(s_a, D_MODEL), x_sample.reshape(s_b, D_MODEL)], axis=0)
    t = s_a + s_b

    tb = _largest_tile(min(s_a, s_b), 512)
    n_a = s_a // tb
    tm_in = _largest_tile(t, 1024)
    tm_merge = _largest_tile(t, 256)

    w_in_b = w_in.astype(BF16)
    w_dr_b = w_down_r.astype(BF16)
    w_dh_b = w_down_h.astype(BF16)
    w_o_b = w_out.astype(BF16)
    w_cat = jnp.concatenate([rg_wa, rg_wx], axis=-1).astype(BF16)
    gain3 = norm_gain.reshape(depth, 1, D_MODEL)
    conv_b3 = conv_b.reshape(depth, 1, D_MODEL)
    ba4 = rg_ba.reshape(depth, 2, 1, D_MODEL)
    bx4 = rg_bx.reshape(depth, 2, 1, D_MODEL)
    lam4 = rg_lambda.reshape(depth, 2, 1, D_MODEL)
    hgain3 = hg_norm_gain.reshape(depth, 1, D_MODEL)
    fgain = final_gain.reshape(1, D_MODEL)

    for l in range(depth):
        proj = _inproj(x, gain3, w_in_b, l, tm_in, 1024)
        hf = _rglru(proj, None, conv_w, conv_b3, w_cat, ba4, bx4, lam4, l, n_a, False, tb, 256, 128)
        a_r = _rglru(proj, hf, conv_w, conv_b3, w_cat, ba4, bx4, lam4, l, n_a, True, tb, 256, 128)
        of = _hgrn(proj, None, hg_lower, hgain3, l, n_a, False, tb, 256)
        a_h = _hgrn(proj, of, hg_lower, hgain3, l, n_a, True, tb, 256)
        x = _merge(x, a_r, a_h, proj, w_dr_b, w_dh_b, w_o_b, fgain, l, l == depth - 1, tm_merge)

    return (x[:s_a].reshape(x_prompt.shape), x[s_a:].reshape(x_sample.shape))
```

```python
import functools

import numpy as np
import jax
import jax.numpy as jnp
from jax import lax
from jax.experimental import pallas as pl
from jax.experimental.pallas import tpu as pltpu

F32 = jnp.float32
BF16 = jnp.bfloat16

D_MODEL = 2048
HEAD = 128
N_SPLITS = 9
IN_COLS = N_SPLITS * D_MODEL
COL_XR, COL_GR, COL_Q, COL_ZF, COL_ZB, COL_V, COL_GH, COL_MR, COL_MH = range(N_SPLITS)
EPS = 1e-6
RG_C = 8.0
CHUNK = 64
LEVELS = (1, 2, 4, 8, 16, 32)
LOG2E = 1.4426950408889634
HALO = 16
VMEM_LIMIT = 56 * 1024 * 1024

_NT = (((1,), (1,)), ((), ()))
_TN = (((0,), (0,)), ((), ()))


def _sigmoid(x):
    return 0.5 * jnp.tanh(0.5 * x) + 0.5


def _silu(x):
    hx = 0.5 * x
    return hx * (1.0 + jnp.tanh(hx))


def _softplus(x):
    return jnp.maximum(x, 0.0) + jnp.log1p(jnp.exp(-jnp.abs(x)))


def _split3(x):
    hi = x.astype(BF16)
    r1 = x - hi.astype(F32)
    mid = r1.astype(BF16)
    lo = (r1 - mid.astype(F32)).astype(BF16)
    return hi, mid, lo


def _largest_tile(n, cap):
    t = cap
    while n % t:
        t //= 2
    return t


def _inproj_body(x_ref, g_ref, w_ref, o_ref, h_ref, *, tm):
    @pl.when(pl.program_id(1) == 0)
    def _():
        rows = 128
        def norm(i, c):
            sl = pl.ds(pl.multiple_of(i * rows, rows), rows)
            x = x_ref[sl, :]
            ms = jnp.mean(x * x, axis=-1, keepdims=True)
            h_ref[sl, :] = (x * lax.rsqrt(ms + EPS) * g_ref[...]).astype(BF16)
            return c
        lax.fori_loop(0, tm // rows, norm, 0)

    o_ref[...] = jnp.dot(h_ref[...], w_ref[...], preferred_element_type=F32).astype(o_ref.dtype)


def _inproj(x, gain, w_in, layer, tm, tn):
    t = x.shape[0]
    return pl.pallas_call(
        functools.partial(_inproj_body, tm=tm),
        grid=(t // tm, IN_COLS // tn),
        in_specs=[
            pl.BlockSpec((tm, D_MODEL), lambda i, j: (i, 0)),
            pl.BlockSpec((None, 1, D_MODEL), lambda i, j: (layer, 0, 0)),
            pl.BlockSpec((None, D_MODEL, tn), lambda i, j: (layer, 0, j)),
        ],
        out_specs=pl.BlockSpec((tm, tn), lambda i, j: (i, j)),
        out_shape=jax.ShapeDtypeStruct((t, IN_COLS), BF16),
        scratch_shapes=[pltpu.VMEM((tm, D_MODEL), BF16)],
        compiler_params=pltpu.CompilerParams(
            dimension_semantics=("arbitrary", "arbitrary"), vmem_limit_bytes=VMEM_LIMIT),
        name="inproj",
    )(x, gain, w_in)


def _rglru_body(*refs, reverse, n_a, n_t, tb, cb, rc):
    if reverse:
        (x_ref, xp_ref, xn_ref, gr_ref, hf_ref, cw_ref, cbias_ref, w_ref, ba_ref, bx_ref,
         lam_ref, o_ref, xs_ref, a_s, h_s, carry_ref) = refs
    else:
        (x_ref, xp_ref, xn_ref, cw_ref, cbias_ref, w_ref, ba_ref, bx_ref,
         lam_ref, o_ref, xs_ref, a_s, h_s, carry_ref) = refs
    nh = cb // HEAD
    nchunk = tb // rc
    nv = rc // 8
    t = pl.program_id(1)
    tix = (n_t - 1 - t) if reverse else t
    seq_first = jnp.logical_or(tix == 0, tix == n_a)
    seq_last = jnp.logical_or(tix == n_a - 1, tix == n_t - 1)
    start = seq_last if reverse else seq_first

    @pl.when(start)
    def _():
        carry_ref[...] = jnp.zeros_like(carry_ref)

    pm = jnp.where(seq_first, 0.0, 1.0).astype(F32)
    nm = jnp.where(seq_last, 0.0, 1.0).astype(F32)
    xs_ref[0:8, :] = xp_ref[...].astype(F32)[HALO - 8:HALO, :] * pm
    xs_ref[8:8 + tb, :] = x_ref[...].astype(F32)
    xs_ref[8 + tb:16 + tb, :] = xn_ref[...].astype(F32)[0:8, :] * nm

    neg_c_sp = (-RG_C * LOG2E) * _softplus(-lam_ref[...])
    sub = lax.broadcasted_iota(jnp.int32, (nv, 8, HEAD), 1)

    def chunk(k, c):
        kk = (nchunk - 1 - k) if reverse else k
        r0 = pl.multiple_of(kk * rc, rc)
        for h in range(nh):
            hs = slice(h * HEAD, (h + 1) * HEAD)
            xw = xs_ref[pl.ds(r0, rc + 16), hs].reshape(nv + 2, 8, HEAD)

            def tap(s):
                if s == 0:
                    return xw[1:nv + 1]
                rt = pltpu.roll(xw, s % 8, axis=1)
                if s > 0:
                    return jnp.where(sub < s, rt[0:nv], rt[1:nv + 1])
                return jnp.where(sub < 8 + s, rt[1:nv + 1], rt[2:nv + 2])

            xc = (cbias_ref[:, hs] + tap(2) * cw_ref[0:1, hs] + tap(1) * cw_ref[1:2, hs]
                  + tap(0) * cw_ref[2:3, hs] + tap(-1) * cw_ref[3:4, hs]).reshape(rc, HEAD)
            g = jnp.dot(xc.astype(BF16), w_ref[h], preferred_element_type=F32)
            r = _sigmoid(g[:, :HEAD] + ba_ref[:, hs])
            i = _sigmoid(g[:, HEAD:] + bx_ref[:, hs])
            a = jnp.exp2(r * neg_c_sp[:, hs])
            bv = jnp.sqrt(1.0 - a * a) * (i * xc)
            a = a.reshape(nv, 8, HEAD)
            bv = bv.reshape(nv, 8, HEAD)
            for d in (1, 2, 4):
                a_sh = pltpu.roll(a, (8 - d) if reverse else d, axis=1)
                b_sh = pltpu.roll(bv, (8 - d) if reverse else d, axis=1)
                valid = (sub < 8 - d) if reverse else (sub >= d)
                bv = jnp.where(valid, a * b_sh + bv, bv)
                a = jnp.where(valid, a * a_sh, a)
            a_s[h] = a.reshape(rc, HEAD)
            h_s[h] = bv.reshape(rc, HEAD)

        def step(i, cs):
            ii = (nv - 1 - i) if reverse else i
            sl = pl.ds(pl.multiple_of(ii * 8, 8), 8)
            out = []
            for h in range(nh):
                hv = h_s[h, sl, :] + a_s[h, sl, :] * cs[h]
                h_s[h, sl, :] = hv
                out.append(hv[0:1, :] if reverse else hv[7:8, :])
            return tuple(out)

        cs = tuple(carry_ref[:, h * HEAD:(h + 1) * HEAD] for h in range(nh))
        cs = lax.fori_loop(0, nv, step, cs, unroll=True)
        for h in range(nh):
            hs = slice(h * HEAD, (h + 1) * HEAD)
            carry_ref[:, hs] = cs[h]
            rows = pl.ds(r0, rc)
            if reverse:
                gate = gr_ref[rows, hs].astype(F32)
                o_ref[rows, hs] = ((hf_ref[rows, hs] + h_s[h]) * _silu(gate)).astype(o_ref.dtype)
            else:
                o_ref[rows, hs] = h_s[h]
        return c

    lax.fori_loop(0, nchunk, chunk, 0)


def _rglru(proj, hf, conv_w, conv_b, w_cat, ba, bx, lam, layer, n_a, reverse, tb, cb, rc):
    t = proj.shape[0]
    n_t = t // tb
    ncg = D_MODEL // cb
    hb = tb // HALO
    nhalo = t // HALO
    d = 1 if reverse else 0

    def tmap(ti):
        return (n_t - 1 - ti) if reverse else ti

    blk = lambda col: pl.BlockSpec((tb, cb), lambda c, ti: (tmap(ti), col * ncg + c))
    vec = lambda: pl.BlockSpec((None, 1, cb), lambda c, ti: (layer, 0, c))
    dvec = lambda: pl.BlockSpec((None, None, 1, cb), lambda c, ti: (layer, d, 0, c))
    in_specs = [
        blk(COL_XR),
        pl.BlockSpec((HALO, cb), lambda c, ti: (jnp.maximum(tmap(ti) * hb - 1, 0), c)),
        pl.BlockSpec((HALO, cb), lambda c, ti: (jnp.minimum((tmap(ti) + 1) * hb, nhalo - 1), c)),
    ]
    args = [proj, proj, proj]
    if reverse:
        in_specs += [blk(COL_GR), pl.BlockSpec((tb, cb), lambda c, ti: (tmap(ti), c))]
        args += [proj, hf]
    in_specs += [
        pl.BlockSpec((None, 4, cb), lambda c, ti: (layer, 0, c)),
        vec(),
        pl.BlockSpec((None, None, cb // HEAD, HEAD, 2 * HEAD), lambda c, ti: (layer, d, c, 0, 0)),
        dvec(), dvec(), dvec(),
    ]
    args += [conv_w, conv_b, w_cat, ba, bx, lam]
    nh = cb // HEAD
    return pl.pallas_call(
        functools.partial(_rglru_body, reverse=reverse, n_a=n_a, n_t=n_t, tb=tb, cb=cb, rc=rc),
        grid=(ncg, n_t),
        in_specs=in_specs,
        out_specs=pl.BlockSpec((tb, cb), lambda c, ti: (tmap(ti), c)),
        out_shape=jax.ShapeDtypeStruct((t, D_MODEL), BF16 if reverse else F32),
        scratch_shapes=[
            pltpu.VMEM((tb + 16, cb), F32),
            pltpu.VMEM((nh, rc, HEAD), F32),
            pltpu.VMEM((nh, rc, HEAD), F32),
            pltpu.VMEM((1, cb), F32),
        ],
        compiler_params=pltpu.CompilerParams(
            dimension_semantics=("arbitrary", "arbitrary"), vmem_limit_bytes=VMEM_LIMIT),
        name="rglru_bwd" if reverse else "rglru_fwd",
    )(*args)


def _decay_sum_matrix(reverse):
    t = np.arange(CHUNK)[:, None]
    u = np.arange(CHUNK)[None, :]
    if reverse:
        blocks = [u >= t, u < t]
    else:
        blocks = [u <= t, u > t]
    for m in LEVELS:
        upper = (t & m) != 0
        if reverse:
            rho = (t // (2 * m)) * (2 * m) + m
            blocks.append(np.where(upper, (u >= rho) & (u < t), (u >= t) & (u < rho)))
        else:
            rho = (t // (2 * m)) * (2 * m) + m - 1
            blocks.append(np.where(upper, (u > rho) & (u <= t), (u > t) & (u <= rho)))
    mat = np.concatenate(blocks, axis=0).astype(np.float32)
    return np.concatenate([mat, mat, mat], axis=1)


def _hgrn_body(*refs, reverse, layer, n_a, n_t, tb, cb):
    if reverse:
        (q_ref, z_ref, v_ref, g_ref, of_ref, lower_ref, gain_ref, m_ref, o_ref, st_ref) = refs
    else:
        (q_ref, z_ref, v_ref, lower_ref, m_ref, o_ref, st_ref) = refs
    nh = cb // HEAD
    nchunk = tb // CHUNK
    t = pl.program_id(1)
    tix = (n_t - 1 - t) if reverse else t
    if reverse:
        start = jnp.logical_or(tix == n_a - 1, tix == n_t - 1)
    else:
        start = jnp.logical_or(tix == 0, tix == n_a)

    @pl.when(start)
    def _():
        st_ref[...] = jnp.zeros_like(st_ref)

    low = lower_ref[...]
    e = jnp.exp(low - jnp.max(low, axis=0, keepdims=True))
    den = jnp.sum(e, axis=0, keepdims=True)
    if layer == 0:
        lb = jnp.zeros_like(den)
    else:
        lb = jnp.sum(e[1:layer + 1, :], axis=0, keepdims=True) / den
    log_lb = jnp.log(lb)
    log1m_lb = jnp.log1p(-lb)
    half_1m_lb = 0.5 * (1.0 - lb)

    ti = lax.broadcasted_iota(jnp.int32, (CHUNK, CHUNK), 0)
    si = lax.broadcasted_iota(jnp.int32, (CHUNK, CHUNK), 1)
    txs = ti ^ si
    causal = (ti <= si) if reverse else (ti >= si)
    rowi = lax.broadcasted_iota(jnp.int32, (CHUNK, HEAD), 0)
    q_side = [((rowi & m) == 0) if reverse else ((rowi & m) != 0) for m in LEVELS]

    def chunk(k, c):
        kk = (nchunk - 1 - k) if reverse else k
        rows = pl.ds(pl.multiple_of(kk * CHUNK, CHUNK), CHUNK)
        hsl = [slice(h * HEAD, (h + 1) * HEAD) for h in range(nh)]
        qk = []
        dall = []
        for pair in range(nh // 2):
            parts = []
            for h in (2 * pair, 2 * pair + 1):
                hs = hsl[h]
                z = z_ref[rows, hs].astype(F32)
                q = _silu(q_ref[rows, hs].astype(F32))
                ez = jnp.exp(-jnp.abs(z))
                log_sig = jnp.minimum(z, 0.0) - jnp.log(1.0 + ez)
                if layer == 0:
                    logf = log_sig
                else:
                    u = log1m_lb[:, hs] + log_sig
                    la = log_lb[:, hs]
                    logf = jnp.maximum(la, u) + jnp.log(1.0 + jnp.exp(-jnp.abs(la - u)))
                key = half_1m_lb[:, hs] - half_1m_lb[:, hs] * jnp.tanh(0.5 * z)
                qk.append((q, key))
                parts.append(_split3(logf * LOG2E))
            rhs = jnp.concatenate(
                [jnp.concatenate([parts[0][j], parts[1][j]], axis=1) for j in range(3)], axis=0)
            dpair = jnp.dot(m_ref[...], rhs, preferred_element_type=F32)
            dall += [dpair[:, :HEAD], dpair[:, HEAD:]]

        levels = []
        o_inter = []
        for h in range(nh):
            q, key = qk[h]
            d = dall[h]
            q16 = q.astype(BF16)
            k16 = key.astype(BF16)
            p = [lax.dot_general(q16, k16, _NT, preferred_element_type=F32)]
            for li in range(len(LEVELS)):
                d_l = d[(2 + li) * CHUNK:(3 + li) * CHUNK, :]
                w_l = jnp.where(q_side[li], q16, k16) * jnp.exp2(d_l.astype(BF16))
                p.append(lax.dot_general(w_l, w_l, _NT, preferred_element_type=F32))
            levels.append(p)

            b2 = d[0:CHUNK, :]
            st = st_ref[h]
            q_in = (q * jnp.exp2(b2)).astype(BF16)
            o_inter.append(lax.dot_general(q_in, st.astype(BF16), _NT, preferred_element_type=F32))
            kd = (key * jnp.exp2(d[CHUNK:2 * CHUNK, :])).astype(BF16)
            total = b2[0:1, :] if reverse else b2[CHUNK - 1:CHUNK, :]
            st_ref[h] = st * jnp.exp2(total) + lax.dot_general(
                v_ref[rows, hsl[h]], kd, _TN, preferred_element_type=F32)

        for h in range(nh):
            hs = hsl[h]
            scores = levels[h][0]
            for li, m in enumerate(LEVELS):
                scores = jnp.where(txs >= m, levels[h][li + 1], scores)
            scores = jnp.where(causal, scores, 0.0)
            o = o_inter[h] + jnp.dot(scores.astype(BF16), v_ref[rows, hs], preferred_element_type=F32)
            if reverse:
                ot = of_ref[rows, hs] + o
                ms = jnp.mean(ot * ot, axis=-1, keepdims=True)
                y = ot * lax.rsqrt(ms + EPS) * gain_ref[:, hs]
                o_ref[rows, hs] = (y * _silu(g_ref[rows, hs].astype(F32))).astype(o_ref.dtype)
            else:
                o_ref[rows, hs] = o
        return c

    lax.fori_loop(0, nchunk, chunk, 0, unroll=4)


def _hgrn(proj, of, hg_lower, hg_gain, layer, n_a, reverse, tb, cb):
    t = proj.shape[0]
    n_t = t // tb
    ncg = D_MODEL // cb
    depth = hg_lower.shape[1]
    d = 1 if reverse else 0
    msum = jnp.asarray(_decay_sum_matrix(reverse), dtype=BF16)

    def tmap(ti):
        return (n_t - 1 - ti) if reverse else ti

    blk = lambda col: pl.BlockSpec((tb, cb), lambda c, ti: (tmap(ti), col * ncg + c))
    in_specs = [blk(COL_Q), blk(COL_ZB if reverse else COL_ZF), blk(COL_V)]
    args = [proj, proj, proj]
    if reverse:
        in_specs += [blk(COL_GH), pl.BlockSpec((tb, cb), lambda c, ti: (tmap(ti), c))]
        args += [proj, of]
    in_specs.append(pl.BlockSpec((None, depth, cb), lambda c, ti: (d, 0, c)))
    args.append(hg_lower)
    if reverse:
        in_specs.append(pl.BlockSpec((None, 1, cb), lambda c, ti: (layer, 0, c)))
        args.append(hg_gain)
    in_specs.append(pl.BlockSpec(msum.shape, lambda c, ti: (0, 0)))
    args.append(msum)
    return pl.pallas_call(
        functools.partial(_hgrn_body, reverse=reverse, layer=layer, n_a=n_a, n_t=n_t, tb=tb, cb=cb),
        grid=(ncg, n_t),
        in_specs=in_specs,
        out_specs=pl.BlockSpec((tb, cb), lambda c, ti: (tmap(ti), c)),
        out_shape=jax.ShapeDtypeStruct((t, D_MODEL), BF16 if reverse else F32),
        scratch_shapes=[pltpu.VMEM((cb // HEAD, HEAD, HEAD), F32)],
        compiler_params=pltpu.CompilerParams(
            dimension_semantics=("arbitrary", "arbitrary"), vmem_limit_bytes=VMEM_LIMIT),
        name="hgrn_bwd" if reverse else "hgrn_fwd",
    )(*args)


def _merge_body(x_ref, ar_ref, ah_ref, mr_ref, mh_ref, wr_ref, wh_ref, wo_ref, fg_ref, o_ref, *, final):
    yr = jnp.dot(ar_ref[...], wr_ref[...], preferred_element_type=F32)
    yh = jnp.dot(ah_ref[...], wh_ref[...], preferred_element_type=F32)
    merged = _sigmoid(mr_ref[...].astype(F32)) * yr + _sigmoid(mh_ref[...].astype(F32)) * yh
    x = x_ref[...] + jnp.dot(merged.astype(BF16), wo_ref[...], preferred_element_type=F32)
    if final:
        ms = jnp.mean(x * x, axis=-1, keepdims=True)
        x = x * lax.rsqrt(ms + EPS) * fg_ref[...]
    o_ref[...] = x


def _merge(x, a_r, a_h, proj, w_dr, w_dh, w_o, final_gain, layer, final, tm):
    t = x.shape[0]
    row = lambda col: pl.BlockSpec((tm, D_MODEL), lambda i: (i, col))
    wgt = lambda: pl.BlockSpec((None, D_MODEL, D_MODEL), lambda i: (layer, 0, 0),
                               pipeline_mode=pl.Buffered(1))
    return pl.pallas_call(
        functools.partial(_merge_body, final=final),
        grid=(t // tm,),
        in_specs=[row(0), row(0), row(0), row(COL_MR), row(COL_MH), wgt(), wgt(), wgt(),
                  pl.BlockSpec((1, D_MODEL), lambda i: (0, 0))],
        out_specs=row(0),
        out_shape=jax.ShapeDtypeStruct((t, D_MODEL), F32),
        compiler_params=pltpu.CompilerParams(
            dimension_semantics=("arbitrary",), vmem_limit_bytes=VMEM_LIMIT),
        name="merge",
    )(x, a_r, a_h, proj, proj, w_dr, w_dh, w_o, final_gain)


def kernel(x_prompt, x_sample, norm_gain, w_in, conv_w, conv_b, rg_wa, rg_ba, rg_wx, rg_bx, rg_lambda,
           hg_lower, hg_norm_gain, w_down_r, w_down_h, w_out, final_gain):
    depth = w_in.shape[0]
    s_a = x_prompt.shape[0] * x_prompt.shape[1]
    s_b = x_sample.shape[0] * x_sample.shape[1]
    assert x_prompt.shape[0] == 1 and x_sample.shape[0] == 1
    x = jnp.concatenate([x_prompt.reshape(s_a, D_MODEL), x_sample.reshape(s_b, D_MODEL)], axis=0)
    t = s_a + s_b

    tb = _largest_tile(min(s_a, s_b), 512)
    n_a = s_a // tb
    tm_in = _largest_tile(t, 1024)
    tm_merge = _largest_tile(t, 256)

    w_in_b = w_in.astype(BF16)
    w_dr_b = w_down_r.astype(BF16)
    w_dh_b = w_down_h.astype(BF16)
    w_o_b = w_out.astype(BF16)
    w_cat = jnp.concatenate([rg_wa, rg_wx], axis=-1).astype(BF16)
    gain3 = norm_gain.reshape(depth, 1, D_MODEL)
    conv_b3 = conv_b.reshape(depth, 1, D_MODEL)
    ba4 = rg_ba.reshape(depth, 2, 1, D_MODEL)
    bx4 = rg_bx.reshape(depth, 2, 1, D_MODEL)
    lam4 = rg_lambda.reshape(depth, 2, 1, D_MODEL)
    hgain3 = hg_norm_gain.reshape(depth, 1, D_MODEL)
    fgain = final_gain.reshape(1, D_MODEL)

    for l in range(depth):
        proj = _inproj(x, gain3, w_in_b, l, tm_in, 1024)
        hf = _rglru(proj, None, conv_w, conv_b3, w_cat, ba4, bx4, lam4, l, n_a, False, tb, 256, 128)
        a_r = _rglru(proj, hf, conv_w, conv_b3, w_cat, ba4, bx4, lam4, l, n_a, True, tb, 256, 128)
        of = _hgrn(proj, None, hg_lower, hgain3, l, n_a, False, tb, 512)
        a_h = _hgrn(proj, of, hg_lower, hgain3, l, n_a, True, tb, 512)
        x = _merge(x, a_r, a_h, proj, w_dr_b, w_dh_b, w_o_b, fgain, l, l == depth - 1, tm_merge)

    return (x[:s_a].reshape(x_prompt.shape), x[s_a:].reshape(x_sample.shape))
```

```python
import functools

import numpy as np
import jax
import jax.numpy as jnp
from jax import lax
from jax.experimental import pallas as pl
from jax.experimental.pallas import tpu as pltpu

F32 = jnp.float32
BF16 = jnp.bfloat16

D_MODEL = 2048
HEAD = 128
N_SPLITS = 9
COL_XR, COL_GR, COL_Q, COL_ZF, COL_ZB, COL_V, COL_GH, COL_MR, COL_MH = range(N_SPLITS)
EPS = 1e-6
RG_C = 8.0
CHUNK = 64
LEVELS = (1, 2, 4, 8, 16, 32)
LOG2E = 1.4426950408889634
HALO = 16
VMEM_LIMIT = 56 * 1024 * 1024

_NT = (((1,), (1,)), ((), ()))
_TN = (((0,), (0,)), ((), ()))


def _sigmoid(x):
    return 0.5 * jnp.tanh(0.5 * x) + 0.5


def _silu(x):
    hx = 0.5 * x
    return hx * (1.0 + jnp.tanh(hx))


def _softplus(x):
    return jnp.maximum(x, 0.0) + jnp.log1p(jnp.exp(-jnp.abs(x)))


def _split3(x):
    hi = x.astype(BF16)
    r1 = x - hi.astype(F32)
    mid = r1.astype(BF16)
    lo = (r1 - mid.astype(F32)).astype(BF16)
    return hi, mid, lo


def _largest_tile(n, cap):
    t = cap
    while n % t:
        t //= 2
    return t


def _inproj_body(x_ref, g_ref, w_ref, o_ref, h_ref, *, tm):
    @pl.when(pl.program_id(1) == 0)
    def _():
        rows = 128
        def norm(i, c):
            sl = pl.ds(pl.multiple_of(i * rows, rows), rows)
            x = x_ref[sl, :]
            ms = jnp.mean(x * x, axis=-1, keepdims=True)
            h_ref[sl, :] = (x * lax.rsqrt(ms + EPS) * g_ref[...]).astype(BF16)
            return c
        lax.fori_loop(0, tm // rows, norm, 0)

    o_ref[...] = jnp.dot(h_ref[...], w_ref[...], preferred_element_type=F32).astype(o_ref.dtype)


def _inproj(x, gain, w_in, layer, tm, tn):
    t = x.shape[0]
    return pl.pallas_call(
        functools.partial(_inproj_body, tm=tm),
        grid=(t // tm, 2 * D_MODEL // tn),
        in_specs=[
            pl.BlockSpec((tm, D_MODEL), lambda i, j: (i, 0)),
            pl.BlockSpec((None, 1, D_MODEL), lambda i, j: (layer, 0, 0)),
            pl.BlockSpec((None, D_MODEL, tn), lambda i, j: (layer, 0, j)),
        ],
        out_specs=[pl.BlockSpec((tm, tn), lambda i, j: (i, j)),
                   pl.BlockSpec((tm, D_MODEL), lambda i, j: (i, 0))],
        out_shape=[jax.ShapeDtypeStruct((t, 2 * D_MODEL), BF16),
                   jax.ShapeDtypeStruct((t, D_MODEL), BF16)],
        compiler_params=pltpu.CompilerParams(
            dimension_semantics=("arbitrary", "arbitrary"), vmem_limit_bytes=VMEM_LIMIT),
        name="inproj",
    )(x, gain, w_in)


def _rglru_body(*refs, reverse, n_extra, n_a, n_t, tb, cb, rc):
    refs = list(refs)
    x_ref, xp_ref, xn_ref = refs[:3]
    del refs[:3]
    if reverse:
        gr_ref, hf_ref = refs[:2]
        del refs[:2]
    cw_ref, cbias_ref, w_ref, ba_ref, bx_ref, lam_ref, hn_ref = refs[:7]
    wx_refs = refs[7:7 + n_extra]
    o_ref = refs[7 + n_extra]
    px_refs = refs[8 + n_extra:8 + 2 * n_extra]
    xs_ref, a_s, h_s, carry_ref = refs[8 + 2 * n_extra:]

    nh = cb // HEAD
    nchunk = tb // rc
    nv = rc // 8
    t = pl.program_id(1)
    tix = (n_t - 1 - t) if reverse else t
    seq_first = jnp.logical_or(tix == 0, tix == n_a)
    seq_last = jnp.logical_or(tix == n_a - 1, tix == n_t - 1)
    start = seq_last if reverse else seq_first

    @pl.when(start)
    def _():
        carry_ref[...] = jnp.zeros_like(carry_ref)

    pm = jnp.where(seq_first, 0.0, 1.0).astype(F32)
    nm = jnp.where(seq_last, 0.0, 1.0).astype(F32)
    xs_ref[0:8, :] = xp_ref[...].astype(F32)[HALO - 8:HALO, :] * pm
    xs_ref[8:8 + tb, :] = x_ref[...].astype(F32)
    xs_ref[8 + tb:16 + tb, :] = xn_ref[...].astype(F32)[0:8, :] * nm

    neg_c_sp = (-RG_C * LOG2E) * _softplus(-lam_ref[...])
    sub = lax.broadcasted_iota(jnp.int32, (nv, 8, HEAD), 1)

    def chunk(k, c):
        kk = (nchunk - 1 - k) if reverse else k
        r0 = pl.multiple_of(kk * rc, rc)
        rows = pl.ds(r0, rc)
        hn = hn_ref[rows, :]
        extra = list(zip(wx_refs, px_refs))

        def project(pairs):
            for wx_ref, px_ref in pairs:
                px_ref[rows, :] = jnp.dot(hn, wx_ref[...], preferred_element_type=F32).astype(px_ref.dtype)

        for h in range(nh):
            project(extra[h::nh + 1])
            hs = slice(h * HEAD, (h + 1) * HEAD)
            xw = xs_ref[pl.ds(r0, rc + 16), hs].reshape(nv + 2, 8, HEAD)

            def tap(s):
                if s == 0:
                    return xw[1:nv + 1]
                rt = pltpu.roll(xw, s % 8, axis=1)
                if s > 0:
                    return jnp.where(sub < s, rt[0:nv], rt[1:nv + 1])
                return jnp.where(sub < 8 + s, rt[1:nv + 1], rt[2:nv + 2])

            xc = (cbias_ref[:, hs] + tap(2) * cw_ref[0:1, hs] + tap(1) * cw_ref[1:2, hs]
                  + tap(0) * cw_ref[2:3, hs] + tap(-1) * cw_ref[3:4, hs]).reshape(rc, HEAD)
            g = jnp.dot(xc.astype(BF16), w_ref[h], preferred_element_type=F32)
            r = _sigmoid(g[:, :HEAD] + ba_ref[:, hs])
            i = _sigmoid(g[:, HEAD:] + bx_ref[:, hs])
            a = jnp.exp2(r * neg_c_sp[:, hs])
            bv = jnp.sqrt(1.0 - a * a) * (i * xc)
            a = a.reshape(nv, 8, HEAD)
            bv = bv.reshape(nv, 8, HEAD)
            for d in (1, 2, 4):
                a_sh = pltpu.roll(a, (8 - d) if reverse else d, axis=1)
                b_sh = pltpu.roll(bv, (8 - d) if reverse else d, axis=1)
                valid = (sub < 8 - d) if reverse else (sub >= d)
                bv = jnp.where(valid, a * b_sh + bv, bv)
                a = jnp.where(valid, a * a_sh, a)
            a_s[h] = a.reshape(rc, HEAD)
            h_s[h] = bv.reshape(rc, HEAD)

        project(extra[nh::nh + 1])

        def step(i, cs):
            ii = (nv - 1 - i) if reverse else i
            sl = pl.ds(pl.multiple_of(ii * 8, 8), 8)
            out = []
            for h in range(nh):
                hv = h_s[h, sl, :] + a_s[h, sl, :] * cs[h]
                h_s[h, sl, :] = hv
                out.append(hv[0:1, :] if reverse else hv[7:8, :])
            return tuple(out)

        cs = tuple(carry_ref[:, h * HEAD:(h + 1) * HEAD] for h in range(nh))
        cs = lax.fori_loop(0, nv, step, cs, unroll=True)
        for h in range(nh):
            hs = slice(h * HEAD, (h + 1) * HEAD)
            carry_ref[:, hs] = cs[h]
            if reverse:
                gate = gr_ref[rows, hs].astype(F32)
                o_ref[rows, hs] = ((hf_ref[rows, hs] + h_s[h]) * _silu(gate)).astype(o_ref.dtype)
            else:
                o_ref[rows, hs] = h_s[h]

        return c

    lax.fori_loop(0, nchunk, chunk, 0, unroll=True)


def _rglru(proj, hf, hn, w_in, extra_cols, conv_w, conv_b, w_cat, ba, bx, lam, layer, n_a, reverse,
           tb, cb, rc):
    t = proj.shape[0]
    n_t = t // tb
    ncg = D_MODEL // cb
    hb = tb // HALO
    nhalo = t // HALO
    d = 1 if reverse else 0
    n_extra = len(extra_cols)

    def tmap(ti):
        return (n_t - 1 - ti) if reverse else ti

    blk = lambda col: pl.BlockSpec((tb, cb), lambda c, ti: (tmap(ti), col * ncg + c))
    vec = lambda: pl.BlockSpec((None, 1, cb), lambda c, ti: (layer, 0, c))
    dvec = lambda: pl.BlockSpec((None, None, 1, cb), lambda c, ti: (layer, d, 0, c))
    in_specs = [
        blk(COL_XR),
        pl.BlockSpec((HALO, cb), lambda c, ti: (jnp.maximum(tmap(ti) * hb - 1, 0), c)),
        pl.BlockSpec((HALO, cb), lambda c, ti: (jnp.minimum((tmap(ti) + 1) * hb, nhalo - 1), c)),
    ]
    args = [proj, proj, proj]
    if reverse:
        in_specs += [blk(COL_GR), pl.BlockSpec((tb, cb), lambda c, ti: (tmap(ti), c))]
        args += [proj, hf]
    in_specs += [
        pl.BlockSpec((None, 4, cb), lambda c, ti: (layer, 0, c)),
        vec(),
        pl.BlockSpec((None, None, cb // HEAD, HEAD, 2 * HEAD), lambda c, ti: (layer, d, c, 0, 0)),
        dvec(), dvec(), dvec(),
        pl.BlockSpec((tb, D_MODEL), lambda c, ti: (tmap(ti), 0)),
    ]
    args += [conv_w, conv_b, w_cat, ba, bx, lam, hn]
    for col in extra_cols:
        in_specs.append(pl.BlockSpec((None, D_MODEL, cb), lambda c, ti, col=col: (layer, 0, col * ncg + c)))
        args.append(w_in)
    out_blk = lambda: pl.BlockSpec((tb, cb), lambda c, ti: (tmap(ti), c))
    nh = cb // HEAD
    return pl.pallas_call(
        functools.partial(_rglru_body, reverse=reverse, n_extra=n_extra, n_a=n_a, n_t=n_t,
                          tb=tb, cb=cb, rc=rc),
        grid=(ncg, n_t),
        in_specs=in_specs,
        out_specs=[out_blk() for _ in range(1 + n_extra)],
        out_shape=[jax.ShapeDtypeStruct((t, D_MODEL), BF16 if reverse else F32)]
        + [jax.ShapeDtypeStruct((t, D_MODEL), BF16) for _ in range(n_extra)],
        scratch_shapes=[
            pltpu.VMEM((tb + 16, cb), F32),
            pltpu.VMEM((nh, rc, HEAD), F32),
            pltpu.VMEM((nh, rc, HEAD), F32),
            pltpu.VMEM((1, cb), F32),
        ],
        compiler_params=pltpu.CompilerParams(
            dimension_semantics=("arbitrary", "arbitrary"), vmem_limit_bytes=VMEM_LIMIT),
        name="rglru_bwd" if reverse else "rglru_fwd",
    )(*args)


def _decay_sum_matrix(reverse):
    t = np.arange(CHUNK)[:, None]
    u = np.arange(CHUNK)[None, :]
    if reverse:
        blocks = [u >= t, u < t]
    else:
        blocks = [u <= t, u > t]
    for m in LEVELS:
        upper = (t & m) != 0
        if reverse:
            rho = (t // (2 * m)) * (2 * m) + m
            blocks.append(np.where(upper, (u >= rho) & (u < t), (u >= t) & (u < rho)))
        else:
            rho = (t // (2 * m)) * (2 * m) + m - 1
            blocks.append(np.where(upper, (u > rho) & (u <= t), (u > t) & (u <= rho)))
    mat = np.concatenate(blocks, axis=0).astype(np.float32)
    return np.concatenate([mat, mat, mat], axis=1)


def _hgrn_body(*refs, reverse, layer, n_a, n_t, tb, cb, group):
    if reverse:
        (q_ref, z_ref, v_ref, g_ref, of_ref, lower_ref, gain_ref, m_ref, o_ref, st_ref) = refs
    else:
        (q_ref, z_ref, v_ref, lower_ref, m_ref, o_ref, st_ref) = refs
    nh = cb // HEAD
    nchunk = tb // CHUNK
    t = pl.program_id(1)
    tix = (n_t - 1 - t) if reverse else t
    if reverse:
        start = jnp.logical_or(tix == n_a - 1, tix == n_t - 1)
    else:
        start = jnp.logical_or(tix == 0, tix == n_a)

    @pl.when(start)
    def _():
        st_ref[...] = jnp.zeros_like(st_ref)

    low = lower_ref[...]
    e = jnp.exp(low - jnp.max(low, axis=0, keepdims=True))
    den = jnp.sum(e, axis=0, keepdims=True)
    if layer == 0:
        lb = jnp.zeros_like(den)
    else:
        lb = jnp.sum(e[1:layer + 1, :], axis=0, keepdims=True) / den
    log_lb = jnp.log(lb)
    log1m_lb = jnp.log1p(-lb)
    half_1m_lb = 0.5 * (1.0 - lb)

    ti = lax.broadcasted_iota(jnp.int32, (CHUNK, CHUNK), 0)
    si = lax.broadcasted_iota(jnp.int32, (CHUNK, CHUNK), 1)
    txs = ti ^ si
    causal = (ti <= si) if reverse else (ti >= si)
    rowi = lax.broadcasted_iota(jnp.int32, (CHUNK, HEAD), 0)
    q_side = [((rowi & m) == 0) if reverse else ((rowi & m) != 0) for m in LEVELS]

    hsl = [slice(h * HEAD, (h + 1) * HEAD) for h in range(nh)]
    items = [(j, h) for j in range(group) for h in range(nh)]

    def body(k, c):
        rows = []
        for j in range(group):
            kc = k * group + j
            kk = (nchunk - 1 - kc) if reverse else kc
            rows.append(pl.ds(pl.multiple_of(kk * CHUNK, CHUNK), CHUNK))

        qk = {}
        dall = {}
        for j in range(group):
            for pair in range(nh // 2):
                parts = []
                for h in (2 * pair, 2 * pair + 1):
                    hs = hsl[h]
                    z = z_ref[rows[j], hs].astype(F32)
                    q = _silu(q_ref[rows[j], hs].astype(F32))
                    ez = jnp.exp(-jnp.abs(z))
                    log_sig = jnp.minimum(z, 0.0) - jnp.log(1.0 + ez)
                    if layer == 0:
                        logf = log_sig
                    else:
                        u = log1m_lb[:, hs] + log_sig
                        la = log_lb[:, hs]
                        logf = jnp.maximum(la, u) + jnp.log(1.0 + jnp.exp(-jnp.abs(la - u)))
                    key = half_1m_lb[:, hs] - half_1m_lb[:, hs] * jnp.tanh(0.5 * z)
                    qk[(j, h)] = (q, key)
                    parts.append(_split3(logf * LOG2E))
                rhs = jnp.concatenate(
                    [jnp.concatenate([parts[0][i], parts[1][i]], axis=1) for i in range(3)], axis=0)
                dpair = jnp.dot(m_ref[...], rhs, preferred_element_type=F32)
                dall[(j, 2 * pair)] = dpair[:, :HEAD]
                dall[(j, 2 * pair + 1)] = dpair[:, HEAD:]

        levels = {}
        o_inter = {}
        for j, h in items:
            q, key = qk[(j, h)]
            d = dall[(j, h)]
            q16 = q.astype(BF16)
            k16 = key.astype(BF16)
            p = [lax.dot_general(q16, k16, _NT, preferred_element_type=F32)]
            for li in range(len(LEVELS)):
                d_l = d[(2 + li) * CHUNK:(3 + li) * CHUNK, :]
                w_l = jnp.where(q_side[li], q16, k16) * jnp.exp2(d_l.astype(BF16))
                p.append(lax.dot_general(w_l, w_l, _NT, preferred_element_type=F32))
            levels[(j, h)] = p

            b2 = d[0:CHUNK, :]
            st = st_ref[h]
            q_in = (q * jnp.exp2(b2)).astype(BF16)
            o_inter[(j, h)] = lax.dot_general(q_in, st.astype(BF16), _NT, preferred_element_type=F32)
            kd = (key * jnp.exp2(d[CHUNK:2 * CHUNK, :])).astype(BF16)
            total = b2[0:1, :] if reverse else b2[CHUNK - 1:CHUNK, :]
            st_ref[h] = st * jnp.exp2(total) + lax.dot_general(
                v_ref[rows[j], hsl[h]], kd, _TN, preferred_element_type=F32)

        for j, h in items:
            hs = hsl[h]
            scores = levels[(j, h)][0]
            for li, m in enumerate(LEVELS):
                scores = jnp.where(txs >= m, levels[(j, h)][li + 1], scores)
            scores = jnp.where(causal, scores, 0.0)
            o = o_inter[(j, h)] + jnp.dot(scores.astype(BF16), v_ref[rows[j], hs],
                                          preferred_element_type=F32)
            if reverse:
                ot = of_ref[rows[j], hs] + o
                ms = jnp.mean(ot * ot, axis=-1, keepdims=True)
                y = ot * lax.rsqrt(ms + EPS) * gain_ref[:, hs]
                o_ref[rows[j], hs] = (y * _silu(g_ref[rows[j], hs].astype(F32))).astype(o_ref.dtype)
            else:
                o_ref[rows[j], hs] = o
        return c

    lax.fori_loop(0, nchunk // group, body, 0)


def _hgrn(q, z, v, gate, of, hg_lower, hg_gain, layer, n_a, reverse, tb, cb, group):
    t = q.shape[0]
    n_t = t // tb
    depth = hg_lower.shape[1]
    d = 1 if reverse else 0
    msum = jnp.asarray(_decay_sum_matrix(reverse), dtype=BF16)

    def tmap(ti):
        return (n_t - 1 - ti) if reverse else ti

    blk = lambda: pl.BlockSpec((tb, cb), lambda c, ti: (tmap(ti), c))
    in_specs = [blk(), blk(), blk()]
    args = [q, z, v]
    if reverse:
        in_specs += [blk(), blk()]
        args += [gate, of]
    in_specs.append(pl.BlockSpec((None, depth, cb), lambda c, ti: (d, 0, c)))
    args.append(hg_lower)
    if reverse:
        in_specs.append(pl.BlockSpec((None, 1, cb), lambda c, ti: (layer, 0, c)))
        args.append(hg_gain)
    in_specs.append(pl.BlockSpec(msum.shape, lambda c, ti: (0, 0)))
    args.append(msum)
    return pl.pallas_call(
        functools.partial(_hgrn_body, reverse=reverse, layer=layer, n_a=n_a, n_t=n_t, tb=tb, cb=cb,
                          group=group),
        grid=(D_MODEL // cb, n_t),
        in_specs=in_specs,
        out_specs=blk(),
        out_shape=jax.ShapeDtypeStruct((t, D_MODEL), BF16 if reverse else F32),
        scratch_shapes=[pltpu.VMEM((cb // HEAD, HEAD, HEAD), F32)],
        compiler_params=pltpu.CompilerParams(
            dimension_semantics=("arbitrary", "arbitrary"), vmem_limit_bytes=VMEM_LIMIT),
        name="hgrn_bwd" if reverse else "hgrn_fwd",
    )(*args)


def _merge_body(x_ref, ar_ref, ah_ref, mr_ref, mh_ref, wr_ref, wh_ref, wo_ref, fg_ref, o_ref, *, final):
    yr = jnp.dot(ar_ref[...], wr_ref[...], preferred_element_type=F32)
    yh = jnp.dot(ah_ref[...], wh_ref[...], preferred_element_type=F32)
    merged = _sigmoid(mr_ref[...].astype(F32)) * yr + _sigmoid(mh_ref[...].astype(F32)) * yh
    x = x_ref[...] + jnp.dot(merged.astype(BF16), wo_ref[...], preferred_element_type=F32)
    if final:
        ms = jnp.mean(x * x, axis=-1, keepdims=True)
        x = x * lax.rsqrt(ms + EPS) * fg_ref[...]
    o_ref[...] = x


def _merge(x, a_r, a_h, m_r, m_h, w_dr, w_dh, w_o, final_gain, layer, final, tm):
    t = x.shape[0]
    row = lambda: pl.BlockSpec((tm, D_MODEL), lambda i: (i, 0))
    wgt = lambda: pl.BlockSpec((None, D_MODEL, D_MODEL), lambda i: (layer, 0, 0),
                               pipeline_mode=pl.Buffered(1))
    return pl.pallas_call(
        functools.partial(_merge_body, final=final),
        grid=(t // tm,),
        in_specs=[row(), row(), row(), row(), row(), wgt(), wgt(), wgt(),
                  pl.BlockSpec((1, D_MODEL), lambda i: (0, 0))],
        out_specs=row(),
        out_shape=jax.ShapeDtypeStruct((t, D_MODEL), F32),
        compiler_params=pltpu.CompilerParams(
            dimension_semantics=("arbitrary",), vmem_limit_bytes=VMEM_LIMIT),
        name="merge",
    )(x, a_r, a_h, m_r, m_h, w_dr, w_dh, w_o, final_gain)


def kernel(x_prompt, x_sample, norm_gain, w_in, conv_w, conv_b, rg_wa, rg_ba, rg_wx, rg_bx, rg_lambda,
           hg_lower, hg_norm_gain, w_down_r, w_down_h, w_out, final_gain):
    depth = w_in.shape[0]
    s_a = x_prompt.shape[0] * x_prompt.shape[1]
    s_b = x_sample.shape[0] * x_sample.shape[1]
    assert x_prompt.shape[0] == 1 and x_sample.shape[0] == 1
    x = jnp.concatenate([x_prompt.reshape(s_a, D_MODEL), x_sample.reshape(s_b, D_MODEL)], axis=0)
    t = s_a + s_b

    tb = _largest_tile(min(s_a, s_b), 512)
    n_a = s_a // tb
    tm_in = _largest_tile(t, 1024)
    tm_merge = _largest_tile(t, 256)

    w_in_b = w_in.astype(BF16)
    w_dr_b = w_down_r.astype(BF16)
    w_dh_b = w_down_h.astype(BF16)
    w_o_b = w_out.astype(BF16)
    w_cat = jnp.concatenate([rg_wa, rg_wx], axis=-1).astype(BF16)
    gain3 = norm_gain.reshape(depth, 1, D_MODEL)
    conv_b3 = conv_b.reshape(depth, 1, D_MODEL)
    ba4 = rg_ba.reshape(depth, 2, 1, D_MODEL)
    bx4 = rg_bx.reshape(depth, 2, 1, D_MODEL)
    lam4 = rg_lambda.reshape(depth, 2, 1, D_MODEL)
    hgain3 = hg_norm_gain.reshape(depth, 1, D_MODEL)
    fgain = final_gain.reshape(1, D_MODEL)

    for l in range(depth):
        proj, hn = _inproj(x, gain3, w_in_b, l, tm_in, 1024)
        rg = (conv_w, conv_b3, w_cat, ba4, bx4, lam4, l, n_a)
        hf, q, zf, v = _rglru(proj, None, hn, w_in_b, (COL_Q, COL_ZF, COL_V), *rg, False, tb, 256, 128)
        a_r, zb, gh, m_r, m_h = _rglru(proj, hf, hn, w_in_b, (COL_ZB, COL_GH, COL_MR, COL_MH), *rg,
                                       True, tb, 256, 128)
        of = _hgrn(q, zf, v, None, None, hg_lower, hgain3, l, n_a, False, tb, 512, 4)
        a_h = _hgrn(q, zb, v, gh, of, hg_lower, hgain3, l, n_a, True, tb, 512, 4)
        x = _merge(x, a_r, a_h, m_r, m_h, w_dr_b, w_dh_b, w_o_b, fgain, l, l == depth - 1, tm_merge)

    return (x[:s_a].reshape(x_prompt.shape), x[s_a:].reshape(x_sample.shape))
```

```python
import functools

import numpy as np
import jax
import jax.numpy as jnp
from jax import lax
from jax.experimental import pallas as pl
from jax.experimental.pallas import tpu as pltpu

F32 = jnp.float32
BF16 = jnp.bfloat16

D_MODEL = 2048
HEAD = 128
N_SPLITS = 9
COL_XR, COL_GR, COL_Q, COL_ZF, COL_ZB, COL_V, COL_GH, COL_MR, COL_MH = range(N_SPLITS)
EPS = 1e-6
RG_C = 8.0
CHUNK = 64
LEVELS = (1, 2, 4, 8, 16, 32)
LOG2E = 1.4426950408889634
HALO = 16
VMEM_LIMIT = 56 * 1024 * 1024

_NT = (((1,), (1,)), ((), ()))
_TN = (((0,), (0,)), ((), ()))


def _sigmoid(x):
    return 0.5 * jnp.tanh(0.5 * x) + 0.5


def _silu(x):
    hx = 0.5 * x
    return hx * (1.0 + jnp.tanh(hx))


def _softplus(x):
    return jnp.maximum(x, 0.0) + jnp.log1p(jnp.exp(-jnp.abs(x)))


def _split3(x):
    hi = x.astype(BF16)
    r1 = x - hi.astype(F32)
    mid = r1.astype(BF16)
    lo = (r1 - mid.astype(F32)).astype(BF16)
    return hi, mid, lo


def _largest_tile(n, cap):
    t = cap
    while n % t:
        t //= 2
    return t


def _row_specs(x, tm, off=0):
    if not isinstance(x, tuple):
        return [pl.BlockSpec((tm, D_MODEL), lambda i, *_: (i + off, 0))], [x]
    n_first = x[0].shape[0] // tm
    return [pl.BlockSpec((tm, D_MODEL), lambda i, *_: (jnp.minimum(i + off, n_first - 1), 0)),
            pl.BlockSpec((tm, D_MODEL), lambda i, *_: (jnp.maximum(i + off - n_first, 0), 0))], list(x)


def _inproj_body(*refs, tm, n_first):
    x_refs = refs[:-4]
    g_ref, w_ref, o_ref, h_ref = refs[-4:]
    i = pl.program_id(0)

    def normalise(x_ref):
        rows = 128
        def norm(r, c):
            sl = pl.ds(pl.multiple_of(r * rows, rows), rows)
            x = x_ref[sl, :]
            ms = jnp.mean(x * x, axis=-1, keepdims=True)
            h_ref[sl, :] = (x * lax.rsqrt(ms + EPS) * g_ref[...]).astype(BF16)
            return c
        lax.fori_loop(0, tm // rows, norm, 0)

    first_col = pl.program_id(1) == 0
    if len(x_refs) == 1:
        pl.when(first_col)(lambda: normalise(x_refs[0]))
    else:
        pl.when(jnp.logical_and(first_col, i < n_first))(lambda: normalise(x_refs[0]))
        pl.when(jnp.logical_and(first_col, i >= n_first))(lambda: normalise(x_refs[1]))

    o_ref[...] = jnp.dot(h_ref[...], w_ref[...], preferred_element_type=F32).astype(o_ref.dtype)


def _inproj(x, t, gain, w_in, layer, tm, tn):
    x_specs, x_args = _row_specs(x, tm)
    n_first = x_args[0].shape[0] // tm
    return pl.pallas_call(
        functools.partial(_inproj_body, tm=tm, n_first=n_first),
        grid=(t // tm, 2 * D_MODEL // tn),
        in_specs=x_specs + [
            pl.BlockSpec((None, 1, D_MODEL), lambda i, j: (layer, 0, 0)),
            pl.BlockSpec((None, D_MODEL, tn), lambda i, j: (layer, 0, j)),
        ],
        out_specs=[pl.BlockSpec((tm, tn), lambda i, j: (i, j)),
                   pl.BlockSpec((tm, D_MODEL), lambda i, j: (i, 0))],
        out_shape=[jax.ShapeDtypeStruct((t, 2 * D_MODEL), BF16),
                   jax.ShapeDtypeStruct((t, D_MODEL), BF16)],
        compiler_params=pltpu.CompilerParams(
            dimension_semantics=("arbitrary", "arbitrary"), vmem_limit_bytes=VMEM_LIMIT),
        name="inproj",
    )(*x_args, gain, w_in)


def _rglru_body(*refs, reverse, n_extra, n_a, n_t, tb, cb, rc, unroll):
    refs = list(refs)
    x_ref, xp_ref, xn_ref = refs[:3]
    del refs[:3]
    if reverse:
        gr_ref, hf_ref = refs[:2]
        del refs[:2]
    cw_ref, cbias_ref, w_ref, ba_ref, bx_ref, lam_ref, hn_ref = refs[:7]
    wx_refs = refs[7:7 + n_extra]
    o_ref = refs[7 + n_extra]
    px_refs = refs[8 + n_extra:8 + 2 * n_extra]
    xs_ref, a_s, h_s, carry_ref, wb_ref = refs[8 + 2 * n_extra:]

    nh = cb // HEAD
    nchunk = tb // rc
    nv = rc // 8
    t = pl.program_id(1)
    tix = (n_t - 1 - t) if reverse else t
    seq_first = jnp.logical_or(tix == 0, tix == n_a)
    seq_last = jnp.logical_or(tix == n_a - 1, tix == n_t - 1)
    start = seq_last if reverse else seq_first

    @pl.when(start)
    def _():
        carry_ref[...] = jnp.zeros_like(carry_ref)

    @pl.when(t == 0)
    def _():
        for e, wx_ref in enumerate(wx_refs):
            wb_ref[e] = wx_ref[...].astype(BF16)

    pm = jnp.where(seq_first, 0.0, 1.0).astype(F32)
    nm = jnp.where(seq_last, 0.0, 1.0).astype(F32)
    xs_ref[0:8, :] = xp_ref[...].astype(F32)[HALO - 8:HALO, :] * pm
    xs_ref[8:8 + tb, :] = x_ref[...].astype(F32)
    xs_ref[8 + tb:16 + tb, :] = xn_ref[...].astype(F32)[0:8, :] * nm

    neg_c_sp = (-RG_C * LOG2E) * _softplus(-lam_ref[...])
    sub = lax.broadcasted_iota(jnp.int32, (nv, 8, HEAD), 1)

    def chunk(k, c):
        kk = (nchunk - 1 - k) if reverse else k
        r0 = pl.multiple_of(kk * rc, rc)
        rows = pl.ds(r0, rc)
        hn = hn_ref[rows, :]
        extra = list(enumerate(px_refs))

        def project(pairs):
            for e, px_ref in pairs:
                px_ref[rows, :] = jnp.dot(hn, wb_ref[e], preferred_element_type=F32).astype(px_ref.dtype)

        for h in range(nh):
            project(extra[h::nh + 1])
            hs = slice(h * HEAD, (h + 1) * HEAD)
            xw = xs_ref[pl.ds(r0, rc + 16), hs].reshape(nv + 2, 8, HEAD)

            def tap(s):
                if s == 0:
                    return xw[1:nv + 1]
                rt = pltpu.roll(xw, s % 8, axis=1)
                if s > 0:
                    return jnp.where(sub < s, rt[0:nv], rt[1:nv + 1])
                return jnp.where(sub < 8 + s, rt[1:nv + 1], rt[2:nv + 2])

            xc = (cbias_ref[:, hs] + tap(2) * cw_ref[0:1, hs] + tap(1) * cw_ref[1:2, hs]
                  + tap(0) * cw_ref[2:3, hs] + tap(-1) * cw_ref[3:4, hs]).reshape(rc, HEAD)
            g = jnp.dot(xc.astype(BF16), w_ref[h], preferred_element_type=F32)
            r = _sigmoid(g[:, :HEAD] + ba_ref[:, hs])
            i = _sigmoid(g[:, HEAD:] + bx_ref[:, hs])
            a = jnp.exp2(r * neg_c_sp[:, hs])
            bv = jnp.sqrt(1.0 - a * a) * (i * xc)
            a = a.reshape(nv, 8, HEAD)
            bv = bv.reshape(nv, 8, HEAD)
            for d in (1, 2, 4):
                a_sh = pltpu.roll(a, (8 - d) if reverse else d, axis=1)
                b_sh = pltpu.roll(bv, (8 - d) if reverse else d, axis=1)
                valid = (sub < 8 - d) if reverse else (sub >= d)
                bv = jnp.where(valid, a * b_sh + bv, bv)
                a = jnp.where(valid, a * a_sh, a)
            a_s[h] = a.reshape(rc, HEAD)
            h_s[h] = bv.reshape(rc, HEAD)

        project(extra[nh::nh + 1])

        def step(i, cs):
            ii = (nv - 1 - i) if reverse else i
            sl = pl.ds(pl.multiple_of(ii * 8, 8), 8)
            out = []
            for h in range(nh):
                hv = h_s[h, sl, :] + a_s[h, sl, :] * cs[h]
                h_s[h, sl, :] = hv
                out.append(hv[0:1, :] if reverse else hv[7:8, :])
            return tuple(out)

        cs = tuple(carry_ref[:, h * HEAD:(h + 1) * HEAD] for h in range(nh))
        cs = lax.fori_loop(0, nv, step, cs, unroll=True)
        for h in range(nh):
            hs = slice(h * HEAD, (h + 1) * HEAD)
            carry_ref[:, hs] = cs[h]
            if reverse:
                gate = gr_ref[rows, hs].astype(F32)
                o_ref[rows, hs] = ((hf_ref[rows, hs] + h_s[h]) * _silu(gate)).astype(o_ref.dtype)
            else:
                o_ref[rows, hs] = h_s[h]

        return c

    lax.fori_loop(0, nchunk, chunk, 0, unroll=unroll)


def _rglru(proj, hf, hn, w_in, extra_cols, conv_w, conv_b, w_cat, ba, bx, lam, layer, n_a, reverse,
           tb, cb, rc, unroll):
    t = proj.shape[0]
    n_t = t // tb
    ncg = D_MODEL // cb
    hb = tb // HALO
    nhalo = t // HALO
    d = 1 if reverse else 0
    n_extra = len(extra_cols)

    def tmap(ti):
        return (n_t - 1 - ti) if reverse else ti

    blk = lambda col: pl.BlockSpec((tb, cb), lambda c, ti: (tmap(ti), col * ncg + c))
    vec = lambda: pl.BlockSpec((None, 1, cb), lambda c, ti: (layer, 0, c))
    dvec = lambda: pl.BlockSpec((None, None, 1, cb), lambda c, ti: (layer, d, 0, c))
    in_specs = [
        blk(COL_XR),
        pl.BlockSpec((HALO, cb), lambda c, ti: (jnp.maximum(tmap(ti) * hb - 1, 0), c)),
        pl.BlockSpec((HALO, cb), lambda c, ti: (jnp.minimum((tmap(ti) + 1) * hb, nhalo - 1), c)),
    ]
    args = [proj, proj, proj]
    if reverse:
        in_specs += [blk(COL_GR), pl.BlockSpec((tb, cb), lambda c, ti: (tmap(ti), c))]
        args += [proj, hf]
    in_specs += [
        pl.BlockSpec((None, 4, cb), lambda c, ti: (layer, 0, c)),
        vec(),
        pl.BlockSpec((None, None, cb // HEAD, HEAD, 2 * HEAD), lambda c, ti: (layer, d, c, 0, 0)),
        dvec(), dvec(), dvec(),
        pl.BlockSpec((tb, D_MODEL), lambda c, ti: (tmap(ti), 0)),
    ]
    args += [conv_w, conv_b, w_cat, ba, bx, lam, hn]
    for col in extra_cols:
        in_specs.append(pl.BlockSpec((None, D_MODEL, cb), lambda c, ti, col=col: (layer, 0, col * ncg + c)))
        args.append(w_in)
    out_blk = lambda: pl.BlockSpec((tb, cb), lambda c, ti: (tmap(ti), c))
    nh = cb // HEAD
    return pl.pallas_call(
        functools.partial(_rglru_body, reverse=reverse, n_extra=n_extra, n_a=n_a, n_t=n_t,
                          tb=tb, cb=cb, rc=rc, unroll=unroll),
        grid=(ncg, n_t),
        in_specs=in_specs,
        out_specs=[out_blk() for _ in range(1 + n_extra)],
        out_shape=[jax.ShapeDtypeStruct((t, D_MODEL), BF16 if reverse else F32)]
        + [jax.ShapeDtypeStruct((t, D_MODEL), BF16) for _ in range(n_extra)],
        scratch_shapes=[
            pltpu.VMEM((tb + 16, cb), F32),
            pltpu.VMEM((nh, rc, HEAD), F32),
            pltpu.VMEM((nh, rc, HEAD), F32),
            pltpu.VMEM((1, cb), F32),
            pltpu.VMEM((n_extra, D_MODEL, cb), BF16),
        ],
        compiler_params=pltpu.CompilerParams(
            dimension_semantics=("arbitrary", "arbitrary"), vmem_limit_bytes=VMEM_LIMIT),
        name="rglru_bwd" if reverse else "rglru_fwd",
    )(*args)


def _decay_sum_matrix(reverse):
    t = np.arange(CHUNK)[:, None]
    u = np.arange(CHUNK)[None, :]
    if reverse:
        blocks = [u >= t, u < t]
    else:
        blocks = [u <= t, u > t]
    for m in LEVELS:
        upper = (t & m) != 0
        if reverse:
            rho = (t // (2 * m)) * (2 * m) + m
            blocks.append(np.where(upper, (u >= rho) & (u < t), (u >= t) & (u < rho)))
        else:
            rho = (t // (2 * m)) * (2 * m) + m - 1
            blocks.append(np.where(upper, (u > rho) & (u <= t), (u > t) & (u <= rho)))
    mat = np.concatenate(blocks, axis=0).astype(np.float32)
    return np.concatenate([mat, mat, mat], axis=1)


def _hgrn_body(*refs, reverse, layer, n_a, n_t, tb, cb, group, unroll):
    if reverse:
        (q_ref, z_ref, v_ref, g_ref, of_ref, lower_ref, gain_ref, m_ref, o_ref, st_ref) = refs
    else:
        (q_ref, z_ref, v_ref, lower_ref, m_ref, o_ref, st_ref) = refs
    nh = cb // HEAD
    nchunk = tb // CHUNK
    t = pl.program_id(1)
    tix = (n_t - 1 - t) if reverse else t
    if reverse:
        start = jnp.logical_or(tix == n_a - 1, tix == n_t - 1)
    else:
        start = jnp.logical_or(tix == 0, tix == n_a)

    @pl.when(start)
    def _():
        st_ref[...] = jnp.zeros_like(st_ref)

    low = lower_ref[...]
    e = jnp.exp(low - jnp.max(low, axis=0, keepdims=True))
    den = jnp.sum(e, axis=0, keepdims=True)
    if layer == 0:
        lb = jnp.zeros_like(den)
    else:
        lb = jnp.sum(e[1:layer + 1, :], axis=0, keepdims=True) / den
    log_lb = jnp.log(lb)
    log1m_lb = jnp.log1p(-lb)
    half_1m_lb = 0.5 * (1.0 - lb)

    ti = lax.broadcasted_iota(jnp.int32, (CHUNK, CHUNK), 0)
    si = lax.broadcasted_iota(jnp.int32, (CHUNK, CHUNK), 1)
    txs = ti ^ si
    causal = (ti <= si) if reverse else (ti >= si)
    rowi = lax.broadcasted_iota(jnp.int32, (CHUNK, HEAD), 0)
    q_side = [((rowi & m) == 0) if reverse else ((rowi & m) != 0) for m in LEVELS]

    hsl = [slice(h * HEAD, (h + 1) * HEAD) for h in range(nh)]
    items = [(j, h) for j in range(group) for h in range(nh)]

    def body(k, c):
        rows = []
        for j in range(group):
            kc = k * group + j
            kk = (nchunk - 1 - kc) if reverse else kc
            rows.append(pl.ds(pl.multiple_of(kk * CHUNK, CHUNK), CHUNK))

        qk = {}
        dall = {}
        for j in range(group):
            for pair in range(nh // 2):
                parts = []
                for h in (2 * pair, 2 * pair + 1):
                    hs = hsl[h]
                    z = z_ref[rows[j], hs].astype(F32)
                    q = _silu(q_ref[rows[j], hs].astype(F32))
                    ez = jnp.exp(-jnp.abs(z))
                    log_sig = jnp.minimum(z, 0.0) - jnp.log(1.0 + ez)
                    if layer == 0:
                        logf = log_sig
                    else:
                        u = log1m_lb[:, hs] + log_sig
                        la = log_lb[:, hs]
                        logf = jnp.maximum(la, u) + jnp.log(1.0 + jnp.exp(-jnp.abs(la - u)))
                    key = half_1m_lb[:, hs] - half_1m_lb[:, hs] * jnp.tanh(0.5 * z)
                    qk[(j, h)] = (q, key)
                    parts.append(_split3(logf * LOG2E))
                rhs = jnp.concatenate(
                    [jnp.concatenate([parts[0][i], parts[1][i]], axis=1) for i in range(3)], axis=0)
                dpair = jnp.dot(m_ref[...], rhs, preferred_element_type=F32)
                dall[(j, 2 * pair)] = dpair[:, :HEAD]
                dall[(j, 2 * pair + 1)] = dpair[:, HEAD:]

        levels = {}
        o_inter = {}
        for j, h in items:
            q, key = qk[(j, h)]
            d = dall[(j, h)]
            q16 = q.astype(BF16)
            k16 = key.astype(BF16)
            p = [lax.dot_general(q16, k16, _NT, preferred_element_type=F32)]
            for li in range(len(LEVELS)):
                d_l = d[(2 + li) * CHUNK:(3 + li) * CHUNK, :]
                w_l = jnp.where(q_side[li], q16, k16) * jnp.exp2(d_l.astype(BF16))
                p.append(lax.dot_general(w_l, w_l, _NT, preferred_element_type=F32))
            levels[(j, h)] = p

            b2 = d[0:CHUNK, :]
            st = st_ref[h]
            q_in = (q * jnp.exp2(b2)).astype(BF16)
            o_inter[(j, h)] = lax.dot_general(q_in, st.astype(BF16), _NT, preferred_element_type=F32)
            kd = (key * jnp.exp2(d[CHUNK:2 * CHUNK, :])).astype(BF16)
            total = b2[0:1, :] if reverse else b2[CHUNK - 1:CHUNK, :]
            st_ref[h] = st * jnp.exp2(total) + lax.dot_general(
                v_ref[rows[j], hsl[h]], kd, _TN, preferred_element_type=F32)

        for j, h in items:
            hs = hsl[h]
            scores = levels[(j, h)][0]
            for li, m in enumerate(LEVELS):
                scores = jnp.where(txs >= m, levels[(j, h)][li + 1], scores)
            scores = jnp.where(causal, scores, 0.0)
            o = o_inter[(j, h)] + jnp.dot(scores.astype(BF16), v_ref[rows[j], hs],
                                          preferred_element_type=F32)
            if reverse:
                ot = of_ref[rows[j], hs] + o
                ms = jnp.mean(ot * ot, axis=-1, keepdims=True)
                y = ot * lax.rsqrt(ms + EPS) * gain_ref[:, hs]
                o_ref[rows[j], hs] = (y * _silu(g_ref[rows[j], hs].astype(F32))).astype(o_ref.dtype)
            else:
                o_ref[rows[j], hs] = o
        return c

    lax.fori_loop(0, nchunk // group, body, 0, unroll=unroll)


def _hgrn(q, z, v, gate, of, hg_lower, hg_gain, layer, n_a, reverse, tb, cb, group, unroll):
    t = q.shape[0]
    n_t = t // tb
    depth = hg_lower.shape[1]
    d = 1 if reverse else 0
    msum = jnp.asarray(_decay_sum_matrix(reverse), dtype=BF16)

    def tmap(ti):
        return (n_t - 1 - ti) if reverse else ti

    blk = lambda: pl.BlockSpec((tb, cb), lambda c, ti: (tmap(ti), c))
    in_specs = [blk(), blk(), blk()]
    args = [q, z, v]
    if reverse:
        in_specs += [blk(), blk()]
        args += [gate, of]
    in_specs.append(pl.BlockSpec((None, depth, cb), lambda c, ti: (d, 0, c)))
    args.append(hg_lower)
    if reverse:
        in_specs.append(pl.BlockSpec((None, 1, cb), lambda c, ti: (layer, 0, c)))
        args.append(hg_gain)
    in_specs.append(pl.BlockSpec(msum.shape, lambda c, ti: (0, 0)))
    args.append(msum)
    return pl.pallas_call(
        functools.partial(_hgrn_body, reverse=reverse, layer=layer, n_a=n_a, n_t=n_t, tb=tb, cb=cb,
                          group=group, unroll=unroll),
        grid=(D_MODEL // cb, n_t),
        in_specs=in_specs,
        out_specs=blk(),
        out_shape=jax.ShapeDtypeStruct((t, D_MODEL), BF16 if reverse else F32),
        scratch_shapes=[pltpu.VMEM((cb // HEAD, HEAD, HEAD), F32)],
        compiler_params=pltpu.CompilerParams(
            dimension_semantics=("arbitrary", "arbitrary"), vmem_limit_bytes=VMEM_LIMIT),
        name="hgrn_bwd" if reverse else "hgrn_fwd",
    )(*args)


def _merge_body(*refs, final, off, n_first):
    x_refs = refs[:-9]
    ar_ref, ah_ref, mr_ref, mh_ref, wr_ref, wh_ref, wo_ref, fg_ref, o_ref = refs[-9:]
    if len(x_refs) == 1:
        x = x_refs[0][...]
    else:
        x = jnp.where(pl.program_id(0) + off < n_first, x_refs[0][...], x_refs[1][...])
    yr = jnp.dot(ar_ref[...], wr_ref[...], preferred_element_type=F32)
    yh = jnp.dot(ah_ref[...], wh_ref[...], preferred_element_type=F32)
    merged = _sigmoid(mr_ref[...].astype(F32)) * yr + _sigmoid(mh_ref[...].astype(F32)) * yh
    x = x + jnp.dot(merged.astype(BF16), wo_ref[...], preferred_element_type=F32)
    if final:
        ms = jnp.mean(x * x, axis=-1, keepdims=True)
        x = x * lax.rsqrt(ms + EPS) * fg_ref[...]
    o_ref[...] = x


def _merge(x, a_r, a_h, m_r, m_h, w_dr, w_dh, w_o, final_gain, layer, final, tm, row0, n_rows):
    off = row0 // tm
    x_specs, x_args = _row_specs(x, tm, off)
    n_first = x_args[0].shape[0] // tm
    row = lambda: pl.BlockSpec((tm, D_MODEL), lambda i: (i + off, 0))
    wgt = lambda: pl.BlockSpec((None, D_MODEL, D_MODEL), lambda i: (layer, 0, 0),
                               pipeline_mode=pl.Buffered(1))
    return pl.pallas_call(
        functools.partial(_merge_body, final=final, off=off, n_first=n_first),
        grid=(n_rows // tm,),
        in_specs=x_specs + [row(), row(), row(), row(), wgt(), wgt(), wgt(),
                            pl.BlockSpec((1, D_MODEL), lambda i: (0, 0))],
        out_specs=pl.BlockSpec((tm, D_MODEL), lambda i: (i, 0)),
        out_shape=jax.ShapeDtypeStruct((n_rows, D_MODEL), F32),
        compiler_params=pltpu.CompilerParams(
            dimension_semantics=("arbitrary",), vmem_limit_bytes=VMEM_LIMIT),
        name="merge",
    )(*x_args, a_r, a_h, m_r, m_h, w_dr, w_dh, w_o, final_gain)


def kernel(x_prompt, x_sample, norm_gain, w_in, conv_w, conv_b, rg_wa, rg_ba, rg_wx, rg_bx, rg_lambda,
           hg_lower, hg_norm_gain, w_down_r, w_down_h, w_out, final_gain):
    depth = w_in.shape[0]
    s_a = x_prompt.shape[0] * x_prompt.shape[1]
    s_b = x_sample.shape[0] * x_sample.shape[1]
    assert x_prompt.shape[0] == 1 and x_sample.shape[0] == 1
    t = s_a + s_b
    x = (x_prompt.reshape(s_a, D_MODEL), x_sample.reshape(s_b, D_MODEL))

    tb = _largest_tile(min(s_a, s_b), 1024)
    n_a = s_a // tb
    tm_in = _largest_tile(min(s_a, s_b), 512)
    tm_merge = _largest_tile(min(s_a, s_b), 256)
    rg_rc = 128
    rg_unroll = min(4, tb // rg_rc)
    hg_group = 4
    hg_unroll = min(2, tb // (CHUNK * hg_group))

    w_rg_b = w_in[:, :, :2 * D_MODEL].astype(BF16)
    w_dr_b = w_down_r.astype(BF16)
    w_dh_b = w_down_h.astype(BF16)
    w_o_b = w_out.astype(BF16)
    w_cat = jnp.concatenate([rg_wa, rg_wx], axis=-1).astype(BF16)
    gain3 = norm_gain.reshape(depth, 1, D_MODEL)
    conv_b3 = conv_b.reshape(depth, 1, D_MODEL)
    ba4 = rg_ba.reshape(depth, 2, 1, D_MODEL)
    bx4 = rg_bx.reshape(depth, 2, 1, D_MODEL)
    lam4 = rg_lambda.reshape(depth, 2, 1, D_MODEL)
    hgain3 = hg_norm_gain.reshape(depth, 1, D_MODEL)
    fgain = final_gain.reshape(1, D_MODEL)

    for l in range(depth):
        proj, hn = _inproj(x, t, gain3, w_rg_b, l, tm_in, 1024)
        rg = (conv_w, conv_b3, w_cat, ba4, bx4, lam4, l, n_a)
        hf, q, zf, v = _rglru(proj, None, hn, w_in, (COL_Q, COL_ZF, COL_V), *rg, False,
                              tb, 256, rg_rc, rg_unroll)
        a_r, zb, gh, m_r, m_h = _rglru(proj, hf, hn, w_in, (COL_ZB, COL_GH, COL_MR, COL_MH), *rg, True,
                                       tb, 256, rg_rc, rg_unroll)
        of = _hgrn(q, zf, v, None, None, hg_lower, hgain3, l, n_a, False, tb, 512, hg_group, hg_unroll)
        a_h = _hgrn(q, zb, v, gh, of, hg_lower, hgain3, l, n_a, True, tb, 512, hg_group, hg_unroll)
        mg = (a_r, a_h, m_r, m_h, w_dr_b, w_dh_b, w_o_b, fgain, l)
        if l < depth - 1:
            x = _merge(x, *mg, False, tm_merge, 0, t)
        else:
            y_a = _merge(x, *mg, True, tm_merge, 0, s_a)
            y_b = _merge(x, *mg, True, tm_merge, s_a, s_b)
    return (y_a.reshape(x_prompt.shape), y_b.reshape(x_sample.shape))
```

```python
import functools

import numpy as np
import jax
import jax.numpy as jnp
from jax import lax
from jax.experimental import pallas as pl
from jax.experimental.pallas import tpu as pltpu

F32 = jnp.float32
BF16 = jnp.bfloat16

D_MODEL = 2048
HEAD = 128
N_SPLITS = 9
COL_XR, COL_GR, COL_Q, COL_ZF, COL_ZB, COL_V, COL_GH, COL_MR, COL_MH = range(N_SPLITS)
EPS = 1e-6
RG_C = 8.0
CHUNK = 64
LEVELS = (1, 2, 4, 8, 16, 32)
LOG2E = 1.4426950408889634
HALO = 16
VMEM_LIMIT = 56 * 1024 * 1024

_NT = (((1,), (1,)), ((), ()))
_TN = (((0,), (0,)), ((), ()))


def _sigmoid(x):
    return 0.5 * jnp.tanh(0.5 * x) + 0.5


def _silu(x):
    hx = 0.5 * x
    return hx * (1.0 + jnp.tanh(hx))


def _softplus(x):
    return jnp.maximum(x, 0.0) + jnp.log1p(jnp.exp(-jnp.abs(x)))


def _split3(x):
    hi = x.astype(BF16)
    r1 = x - hi.astype(F32)
    mid = r1.astype(BF16)
    lo = (r1 - mid.astype(F32)).astype(BF16)
    return hi, mid, lo


def _largest_tile(n, cap):
    t = cap
    while n % t:
        t //= 2
    return t


def _row_specs(x, tm, off=0):
    if not isinstance(x, tuple):
        return [pl.BlockSpec((tm, D_MODEL), lambda i, *_: (i + off, 0))], [x]
    n_first = x[0].shape[0] // tm
    return [pl.BlockSpec((tm, D_MODEL), lambda i, *_: (jnp.minimum(i + off, n_first - 1), 0)),
            pl.BlockSpec((tm, D_MODEL), lambda i, *_: (jnp.maximum(i + off - n_first, 0), 0))], list(x)


def _inproj_body(*refs, tm, n_first):
    x_refs = refs[:-4]
    g_ref, w_ref, o_ref, h_ref = refs[-4:]
    i = pl.program_id(0)

    def normalise(x_ref):
        rows = 128
        def norm(r, c):
            sl = pl.ds(pl.multiple_of(r * rows, rows), rows)
            x = x_ref[sl, :]
            ms = jnp.mean(x * x, axis=-1, keepdims=True)
            h_ref[sl, :] = (x * lax.rsqrt(ms + EPS) * g_ref[...]).astype(BF16)
            return c
        lax.fori_loop(0, tm // rows, norm, 0)

    first_col = pl.program_id(1) == 0
    if len(x_refs) == 1:
        pl.when(first_col)(lambda: normalise(x_refs[0]))
    else:
        pl.when(jnp.logical_and(first_col, i < n_first))(lambda: normalise(x_refs[0]))
        pl.when(jnp.logical_and(first_col, i >= n_first))(lambda: normalise(x_refs[1]))

    o_ref[...] = jnp.dot(h_ref[...], w_ref[...], preferred_element_type=F32).astype(o_ref.dtype)


def _inproj(x, t, gain, w_in, layer, tm, tn):
    x_specs, x_args = _row_specs(x, tm)
    n_first = x_args[0].shape[0] // tm
    return pl.pallas_call(
        functools.partial(_inproj_body, tm=tm, n_first=n_first),
        grid=(t // tm, 2 * D_MODEL // tn),
        in_specs=x_specs + [
            pl.BlockSpec((None, 1, D_MODEL), lambda i, j: (layer, 0, 0)),
            pl.BlockSpec((None, D_MODEL, tn), lambda i, j: (layer, 0, j)),
        ],
        out_specs=[pl.BlockSpec((tm, tn), lambda i, j: (i, j)),
                   pl.BlockSpec((tm, D_MODEL), lambda i, j: (i, 0))],
        out_shape=[jax.ShapeDtypeStruct((t, 2 * D_MODEL), BF16),
                   jax.ShapeDtypeStruct((t, D_MODEL), BF16)],
        compiler_params=pltpu.CompilerParams(
            dimension_semantics=("arbitrary", "arbitrary"), vmem_limit_bytes=VMEM_LIMIT),
        name="inproj",
    )(*x_args, gain, w_in)


def _rglru_body(*refs, reverse, n_extra, n_a, n_t, tb, cb, rc, unroll):
    refs = list(refs)
    x_ref, xp_ref, xn_ref = refs[:3]
    del refs[:3]
    if reverse:
        gr_ref, hf_ref = refs[:2]
        del refs[:2]
    cw_ref, cbias_ref, w_ref, ba_ref, bx_ref, lam_ref, hn_ref = refs[:7]
    wx_refs = refs[7:7 + n_extra]
    o_ref = refs[7 + n_extra]
    px_refs = refs[8 + n_extra:8 + 2 * n_extra]
    xs_ref, a_s, h_s, carry_ref, wb_ref = refs[8 + 2 * n_extra:]

    nh = cb // HEAD
    nchunk = tb // rc
    nv = rc // 8
    t = pl.program_id(1)
    tix = (n_t - 1 - t) if reverse else t
    seq_first = jnp.logical_or(tix == 0, tix == n_a)
    seq_last = jnp.logical_or(tix == n_a - 1, tix == n_t - 1)
    start = seq_last if reverse else seq_first

    @pl.when(start)
    def _():
        carry_ref[...] = jnp.zeros_like(carry_ref)

    @pl.when(t == 0)
    def _():
        for e, wx_ref in enumerate(wx_refs):
            wb_ref[e] = wx_ref[...].astype(BF16)

    pm = jnp.where(seq_first, 0.0, 1.0).astype(F32)
    nm = jnp.where(seq_last, 0.0, 1.0).astype(F32)
    xs_ref[0:8, :] = xp_ref[...].astype(F32)[HALO - 8:HALO, :] * pm
    xs_ref[8:8 + tb, :] = x_ref[...].astype(F32)
    xs_ref[8 + tb:16 + tb, :] = xn_ref[...].astype(F32)[0:8, :] * nm

    neg_c_sp = (-RG_C * LOG2E) * _softplus(-lam_ref[...])
    sub = lax.broadcasted_iota(jnp.int32, (nv, 8, HEAD), 1)

    def chunk(k, c):
        kk = (nchunk - 1 - k) if reverse else k
        r0 = pl.multiple_of(kk * rc, rc)
        rows = pl.ds(r0, rc)
        hn = hn_ref[rows, :]
        extra = list(enumerate(px_refs))

        def project(pairs):
            for e, px_ref in pairs:
                px_ref[rows, :] = jnp.dot(hn, wb_ref[e], preferred_element_type=F32).astype(px_ref.dtype)

        for h in range(nh):
            project(extra[h::nh + 1])
            hs = slice(h * HEAD, (h + 1) * HEAD)
            xw = xs_ref[pl.ds(r0, rc + 16), hs].reshape(nv + 2, 8, HEAD)

            def tap(s):
                if s == 0:
                    return xw[1:nv + 1]
                rt = pltpu.roll(xw, s % 8, axis=1)
                if s > 0:
                    return jnp.where(sub < s, rt[0:nv], rt[1:nv + 1])
                return jnp.where(sub < 8 + s, rt[1:nv + 1], rt[2:nv + 2])

            xc = (cbias_ref[:, hs] + tap(2) * cw_ref[0:1, hs] + tap(1) * cw_ref[1:2, hs]
                  + tap(0) * cw_ref[2:3, hs] + tap(-1) * cw_ref[3:4, hs]).reshape(rc, HEAD)
            g = jnp.dot(xc.astype(BF16), w_ref[h], preferred_element_type=F32)
            r = _sigmoid(g[:, :HEAD] + ba_ref[:, hs])
            i = _sigmoid(g[:, HEAD:] + bx_ref[:, hs])
            a = jnp.exp2(r * neg_c_sp[:, hs])
            bv = jnp.sqrt(1.0 - a * a) * (i * xc)
            a = a.reshape(nv, 8, HEAD)
            bv = bv.reshape(nv, 8, HEAD)
            for d in (1, 2, 4):
                a_sh = pltpu.roll(a, (8 - d) if reverse else d, axis=1)
                b_sh = pltpu.roll(bv, (8 - d) if reverse else d, axis=1)
                valid = (sub < 8 - d) if reverse else (sub >= d)
                bv = jnp.where(valid, a * b_sh + bv, bv)
                a = jnp.where(valid, a * a_sh, a)
            a_s[h] = a.reshape(rc, HEAD)
            h_s[h] = bv.reshape(rc, HEAD)

        project(extra[nh::nh + 1])

        def step(i, cs):
            ii = (nv - 1 - i) if reverse else i
            sl = pl.ds(pl.multiple_of(ii * 8, 8), 8)
            out = []
            for h in range(nh):
                hv = h_s[h, sl, :] + a_s[h, sl, :] * cs[h]
                h_s[h, sl, :] = hv
                out.append(hv[0:1, :] if reverse else hv[7:8, :])
            return tuple(out)

        cs = tuple(carry_ref[:, h * HEAD:(h + 1) * HEAD] for h in range(nh))
        cs = lax.fori_loop(0, nv, step, cs, unroll=True)
        for h in range(nh):
            hs = slice(h * HEAD, (h + 1) * HEAD)
            carry_ref[:, hs] = cs[h]
            if reverse:
                gate = gr_ref[rows, hs].astype(F32)
                o_ref[rows, hs] = ((hf_ref[rows, hs] + h_s[h]) * _silu(gate)).astype(o_ref.dtype)
            else:
                o_ref[rows, hs] = h_s[h]

        return c

    lax.fori_loop(0, nchunk, chunk, 0, unroll=unroll)


def _rglru(proj, hf, hn, w_in, extra_cols, conv_w, conv_b, w_cat, ba, bx, lam, layer, n_a, reverse,
           tb, cb, rc, unroll):
    t = proj.shape[0]
    n_t = t // tb
    ncg = D_MODEL // cb
    hb = tb // HALO
    nhalo = t // HALO
    d = 1 if reverse else 0
    n_extra = len(extra_cols)

    def tmap(ti):
        return (n_t - 1 - ti) if reverse else ti

    blk = lambda col: pl.BlockSpec((tb, cb), lambda c, ti: (tmap(ti), col * ncg + c))
    vec = lambda: pl.BlockSpec((None, 1, cb), lambda c, ti: (layer, 0, c))
    dvec = lambda: pl.BlockSpec((None, None, 1, cb), lambda c, ti: (layer, d, 0, c))
    in_specs = [
        blk(COL_XR),
        pl.BlockSpec((HALO, cb), lambda c, ti: (jnp.maximum(tmap(ti) * hb - 1, 0), c)),
        pl.BlockSpec((HALO, cb), lambda c, ti: (jnp.minimum((tmap(ti) + 1) * hb, nhalo - 1), c)),
    ]
    args = [proj, proj, proj]
    if reverse:
        in_specs += [blk(COL_GR), pl.BlockSpec((tb, cb), lambda c, ti: (tmap(ti), c))]
        args += [proj, hf]
    in_specs += [
        pl.BlockSpec((None, 4, cb), lambda c, ti: (layer, 0, c)),
        vec(),
        pl.BlockSpec((None, None, cb // HEAD, HEAD, 2 * HEAD), lambda c, ti: (layer, d, c, 0, 0)),
        dvec(), dvec(), dvec(),
        pl.BlockSpec((tb, D_MODEL), lambda c, ti: (tmap(ti), 0)),
    ]
    args += [conv_w, conv_b, w_cat, ba, bx, lam, hn]
    for col in extra_cols:
        in_specs.append(pl.BlockSpec((None, D_MODEL, cb), lambda c, ti, col=col: (layer, 0, col * ncg + c),
                                     pipeline_mode=pl.Buffered(1)))
        args.append(w_in)
    out_blk = lambda: pl.BlockSpec((tb, cb), lambda c, ti: (tmap(ti), c))
    nh = cb // HEAD
    return pl.pallas_call(
        functools.partial(_rglru_body, reverse=reverse, n_extra=n_extra, n_a=n_a, n_t=n_t,
                          tb=tb, cb=cb, rc=rc, unroll=unroll),
        grid=(ncg, n_t),
        in_specs=in_specs,
        out_specs=[out_blk() for _ in range(1 + n_extra)],
        out_shape=[jax.ShapeDtypeStruct((t, D_MODEL), BF16 if reverse else F32)]
        + [jax.ShapeDtypeStruct((t, D_MODEL), BF16) for _ in range(n_extra)],
        scratch_shapes=[
            pltpu.VMEM((tb + 16, cb), F32),
            pltpu.VMEM((nh, rc, HEAD), F32),
            pltpu.VMEM((nh, rc, HEAD), F32),
            pltpu.VMEM((1, cb), F32),
            pltpu.VMEM((n_extra, D_MODEL, cb), BF16),
        ],
        compiler_params=pltpu.CompilerParams(
            dimension_semantics=("arbitrary", "arbitrary"), vmem_limit_bytes=VMEM_LIMIT),
        name="rglru_bwd" if reverse else "rglru_fwd",
    )(*args)


def _decay_sum_matrix(reverse):
    t = np.arange(CHUNK)[:, None]
    u = np.arange(CHUNK)[None, :]
    if reverse:
        blocks = [u >= t, u < t]
    else:
        blocks = [u <= t, u > t]
    for m in LEVELS:
        upper = (t & m) != 0
        if reverse:
            rho = (t // (2 * m)) * (2 * m) + m
            blocks.append(np.where(upper, (u >= rho) & (u < t), (u >= t) & (u < rho)))
        else:
            rho = (t // (2 * m)) * (2 * m) + m - 1
            blocks.append(np.where(upper, (u > rho) & (u <= t), (u > t) & (u <= rho)))
    mat = np.concatenate(blocks, axis=0).astype(np.float32)
    return np.concatenate([mat, mat, mat], axis=1)


def _hgrn_body(*refs, reverse, layer, n_a, n_t, tb, cb, group, unroll):
    if reverse:
        (q_ref, z_ref, v_ref, g_ref, of_ref, lower_ref, gain_ref, m_ref, o_ref, st_ref) = refs
    else:
        (q_ref, z_ref, v_ref, lower_ref, m_ref, o_ref, st_ref) = refs
    nh = cb // HEAD
    nchunk = tb // CHUNK
    t = pl.program_id(1)
    tix = (n_t - 1 - t) if reverse else t
    if reverse:
        start = jnp.logical_or(tix == n_a - 1, tix == n_t - 1)
    else:
        start = jnp.logical_or(tix == 0, tix == n_a)

    @pl.when(start)
    def _():
        st_ref[...] = jnp.zeros_like(st_ref)

    low = lower_ref[...]
    e = jnp.exp(low - jnp.max(low, axis=0, keepdims=True))
    den = jnp.sum(e, axis=0, keepdims=True)
    if layer == 0:
        lb = jnp.zeros_like(den)
    else:
        lb = jnp.sum(e[1:layer + 1, :], axis=0, keepdims=True) / den
    log_lb = jnp.log(lb)
    log1m_lb = jnp.log1p(-lb)
    half_1m_lb = 0.5 * (1.0 - lb)

    ti = lax.broadcasted_iota(jnp.int32, (CHUNK, CHUNK), 0)
    si = lax.broadcasted_iota(jnp.int32, (CHUNK, CHUNK), 1)
    txs = ti ^ si
    causal = (ti <= si) if reverse else (ti >= si)
    rowi = lax.broadcasted_iota(jnp.int32, (CHUNK, HEAD), 0)
    q_side = [((rowi & m) == 0) if reverse else ((rowi & m) != 0) for m in LEVELS]

    hsl = [slice(h * HEAD, (h + 1) * HEAD) for h in range(nh)]
    items = [(j, h) for j in range(group) for h in range(nh)]

    def body(k, c):
        rows = []
        for j in range(group):
            kc = k * group + j
            kk = (nchunk - 1 - kc) if reverse else kc
            rows.append(pl.ds(pl.multiple_of(kk * CHUNK, CHUNK), CHUNK))

        qk = {}
        dall = {}
        for j in range(group):
            for pair in range(nh // 2):
                parts = []
                for h in (2 * pair, 2 * pair + 1):
                    hs = hsl[h]
                    z = z_ref[rows[j], hs].astype(F32)
                    q = _silu(q_ref[rows[j], hs].astype(F32))
                    ez = jnp.exp(-jnp.abs(z))
                    log_sig = jnp.minimum(z, 0.0) - jnp.log(1.0 + ez)
                    if layer == 0:
                        logf = log_sig
                    else:
                        u = log1m_lb[:, hs] + log_sig
                        la = log_lb[:, hs]
                        logf = jnp.maximum(la, u) + jnp.log(1.0 + jnp.exp(-jnp.abs(la - u)))
                    key = half_1m_lb[:, hs] - half_1m_lb[:, hs] * jnp.tanh(0.5 * z)
                    qk[(j, h)] = (q, key)
                    parts.append(_split3(logf * LOG2E))
                rhs = jnp.concatenate(
                    [jnp.concatenate([parts[0][i], parts[1][i]], axis=1) for i in range(3)], axis=0)
                dpair = jnp.dot(m_ref[...], rhs, preferred_element_type=F32)
                dall[(j, 2 * pair)] = dpair[:, :HEAD]
                dall[(j, 2 * pair + 1)] = dpair[:, HEAD:]

        levels = {}
        o_inter = {}
        for j, h in items:
            q, key = qk[(j, h)]
            d = dall[(j, h)]
            q16 = q.astype(BF16)
            k16 = key.astype(BF16)
            p = [lax.dot_general(q16, k16, _NT, preferred_element_type=F32)]
            for li in range(len(LEVELS)):
                d_l = d[(2 + li) * CHUNK:(3 + li) * CHUNK, :]
                w_l = jnp.where(q_side[li], q16, k16) * jnp.exp2(d_l.astype(BF16))
                p.append(lax.dot_general(w_l, w_l, _NT, preferred_element_type=F32))
            levels[(j, h)] = p

            b2 = d[0:CHUNK, :]
            st = st_ref[h]
            q_in = (q * jnp.exp2(b2)).astype(BF16)
            o_inter[(j, h)] = lax.dot_general(q_in, st.astype(BF16), _NT, preferred_element_type=F32)
            kd = (key * jnp.exp2(d[CHUNK:2 * CHUNK, :])).astype(BF16)
            total = b2[0:1, :] if reverse else b2[CHUNK - 1:CHUNK, :]
            st_ref[h] = st * jnp.exp2(total) + lax.dot_general(
                v_ref[rows[j], hsl[h]], kd, _TN, preferred_element_type=F32)

        for j, h in items:
            hs = hsl[h]
            scores = levels[(j, h)][0]
            for li, m in enumerate(LEVELS):
                scores = jnp.where(txs >= m, levels[(j, h)][li + 1], scores)
            scores = jnp.where(causal, scores, 0.0)
            o = o_inter[(j, h)] + jnp.dot(scores.astype(BF16), v_ref[rows[j], hs],
                                          preferred_element_type=F32)
            if reverse:
                ot = of_ref[rows[j], hs] + o
                ms = jnp.mean(ot * ot, axis=-1, keepdims=True)
                y = ot * lax.rsqrt(ms + EPS) * gain_ref[:, hs]
                o_ref[rows[j], hs] = (y * _silu(g_ref[rows[j], hs].astype(F32))).astype(o_ref.dtype)
            else:
                o_ref[rows[j], hs] = o
        return c

    lax.fori_loop(0, nchunk // group, body, 0, unroll=unroll)


def _hgrn(q, z, v, gate, of, hg_lower, hg_gain, layer, n_a, reverse, tb, cb, group, unroll):
    t = q.shape[0]
    n_t = t // tb
    depth = hg_lower.shape[1]
    d = 1 if reverse else 0
    msum = jnp.asarray(_decay_sum_matrix(reverse), dtype=BF16)

    def tmap(ti):
        return (n_t - 1 - ti) if reverse else ti

    blk = lambda: pl.BlockSpec((tb, cb), lambda c, ti: (tmap(ti), c))
    in_specs = [blk(), blk(), blk()]
    args = [q, z, v]
    if reverse:
        in_specs += [blk(), blk()]
        args += [gate, of]
    in_specs.append(pl.BlockSpec((None, depth, cb), lambda c, ti: (d, 0, c)))
    args.append(hg_lower)
    if reverse:
        in_specs.append(pl.BlockSpec((None, 1, cb), lambda c, ti: (layer, 0, c)))
        args.append(hg_gain)
    in_specs.append(pl.BlockSpec(msum.shape, lambda c, ti: (0, 0)))
    args.append(msum)
    return pl.pallas_call(
        functools.partial(_hgrn_body, reverse=reverse, layer=layer, n_a=n_a, n_t=n_t, tb=tb, cb=cb,
                          group=group, unroll=unroll),
        grid=(D_MODEL // cb, n_t),
        in_specs=in_specs,
        out_specs=blk(),
        out_shape=jax.ShapeDtypeStruct((t, D_MODEL), BF16 if reverse else F32),
        scratch_shapes=[pltpu.VMEM((cb // HEAD, HEAD, HEAD), F32)],
        compiler_params=pltpu.CompilerParams(
            dimension_semantics=("arbitrary", "arbitrary"), vmem_limit_bytes=VMEM_LIMIT),
        name="hgrn_bwd" if reverse else "hgrn_fwd",
    )(*args)


def _merge_body(*refs, final, off, n_first):
    x_refs = refs[:-9]
    ar_ref, ah_ref, mr_ref, mh_ref, wr_ref, wh_ref, wo_ref, fg_ref, o_ref = refs[-9:]
    if len(x_refs) == 1:
        x = x_refs[0][...]
    else:
        x = jnp.where(pl.program_id(0) + off < n_first, x_refs[0][...], x_refs[1][...])
    yr = jnp.dot(ar_ref[...], wr_ref[...], preferred_element_type=F32)
    yh = jnp.dot(ah_ref[...], wh_ref[...], preferred_element_type=F32)
    merged = _sigmoid(mr_ref[...].astype(F32)) * yr + _sigmoid(mh_ref[...].astype(F32)) * yh
    x = x + jnp.dot(merged.astype(BF16), wo_ref[...], preferred_element_type=F32)
    if final:
        ms = jnp.mean(x * x, axis=-1, keepdims=True)
        x = x * lax.rsqrt(ms + EPS) * fg_ref[...]
    o_ref[...] = x


def _merge(x, a_r, a_h, m_r, m_h, w_dr, w_dh, w_o, final_gain, layer, final, tm, row0, n_rows):
    off = row0 // tm
    x_specs, x_args = _row_specs(x, tm, off)
    n_first = x_args[0].shape[0] // tm
    row = lambda: pl.BlockSpec((tm, D_MODEL), lambda i: (i + off, 0))
    wgt = lambda: pl.BlockSpec((None, D_MODEL, D_MODEL), lambda i: (layer, 0, 0),
                               pipeline_mode=pl.Buffered(1))
    return pl.pallas_call(
        functools.partial(_merge_body, final=final, off=off, n_first=n_first),
        grid=(n_rows // tm,),
        in_specs=x_specs + [row(), row(), row(), row(), wgt(), wgt(), wgt(),
                            pl.BlockSpec((1, D_MODEL), lambda i: (0, 0))],
        out_specs=pl.BlockSpec((tm, D_MODEL), lambda i: (i, 0)),
        out_shape=jax.ShapeDtypeStruct((n_rows, D_MODEL), F32),
        compiler_params=pltpu.CompilerParams(
            dimension_semantics=("arbitrary",), vmem_limit_bytes=VMEM_LIMIT),
        name="merge",
    )(*x_args, a_r, a_h, m_r, m_h, w_dr, w_dh, w_o, final_gain)


def kernel(x_prompt, x_sample, norm_gain, w_in, conv_w, conv_b, rg_wa, rg_ba, rg_wx, rg_bx, rg_lambda,
           hg_lower, hg_norm_gain, w_down_r, w_down_h, w_out, final_gain):
    depth = w_in.shape[0]
    s_a = x_prompt.shape[0] * x_prompt.shape[1]
    s_b = x_sample.shape[0] * x_sample.shape[1]
    assert x_prompt.shape[0] == 1 and x_sample.shape[0] == 1
    t = s_a + s_b
    x = (x_prompt.reshape(s_a, D_MODEL), x_sample.reshape(s_b, D_MODEL))

    tb = _largest_tile(min(s_a, s_b), 1024)
    n_a = s_a // tb
    tm_in = _largest_tile(min(s_a, s_b), 512)
    tm_merge = _largest_tile(min(s_a, s_b), 256)
    rg_cb = 512
    rg_rc = 128
    rg_unroll = min(4, tb // rg_rc)
    hg_group = 4
    hg_unroll = min(4, tb // (CHUNK * hg_group))

    w_rg_b = w_in[:, :, :2 * D_MODEL].astype(BF16)
    w_dr_b = w_down_r.astype(BF16)
    w_dh_b = w_down_h.astype(BF16)
    w_o_b = w_out.astype(BF16)
    w_cat = jnp.concatenate([rg_wa, rg_wx], axis=-1).astype(BF16)
    gain3 = norm_gain.reshape(depth, 1, D_MODEL)
    conv_b3 = conv_b.reshape(depth, 1, D_MODEL)
    ba4 = rg_ba.reshape(depth, 2, 1, D_MODEL)
    bx4 = rg_bx.reshape(depth, 2, 1, D_MODEL)
    lam4 = rg_lambda.reshape(depth, 2, 1, D_MODEL)
    hgain3 = hg_norm_gain.reshape(depth, 1, D_MODEL)
    fgain = final_gain.reshape(1, D_MODEL)

    for l in range(depth):
        proj, hn = _inproj(x, t, gain3, w_rg_b, l, tm_in if l == 0 else _largest_tile(t, 1024), 1024)
        rg = (conv_w, conv_b3, w_cat, ba4, bx4, lam4, l, n_a)
        hf, q, zf, v = _rglru(proj, None, hn, w_in, (COL_Q, COL_ZF, COL_V), *rg, False,
                              tb, rg_cb, rg_rc, rg_unroll)
        a_r, zb, gh, m_r, m_h = _rglru(proj, hf, hn, w_in, (COL_ZB, COL_GH, COL_MR, COL_MH), *rg, True,
                                       tb, rg_cb, rg_rc, rg_unroll)
        of = _hgrn(q, zf, v, None, None, hg_lower, hgain3, l, n_a, False, tb, 512, hg_group, hg_unroll)
        a_h = _hgrn(q, zb, v, gh, of, hg_lower, hgain3, l, n_a, True, tb, 512, hg_group, hg_unroll)
        mg = (a_r, a_h, m_r, m_h, w_dr_b, w_dh_b, w_o_b, fgain, l)
        if l < depth - 1:
            x = _merge(x, *mg, False, tm_merge, 0, t)
        else:
            y_a = _merge(x, *mg, True, tm_merge, 0, s_a)
            y_b = _merge(x, *mg, True, tm_merge, s_a, s_b)
    return (y_a.reshape(x_prompt.shape), y_b.reshape(x_sample.shape))
```

```python
import functools

import numpy as np
import jax
import jax.numpy as jnp
from jax import lax
from jax.experimental import pallas as pl
from jax.experimental.pallas import tpu as pltpu

F32 = jnp.float32
BF16 = jnp.bfloat16

D_MODEL = 2048
HEAD = 128
N_SPLITS = 9
COL_XR, COL_GR, COL_Q, COL_ZF, COL_ZB, COL_V, COL_GH, COL_MR, COL_MH = range(N_SPLITS)
EPS = 1e-6
RG_C = 8.0
CHUNK = 64
LEVELS = (1, 2, 4, 8, 16, 32)
LOG2E = 1.4426950408889634
HALO = 16
VMEM_LIMIT = 56 * 1024 * 1024

_NT = (((1,), (1,)), ((), ()))
_TN = (((0,), (0,)), ((), ()))


def _sigmoid(x):
    return 0.5 * jnp.tanh(0.5 * x) + 0.5


def _silu(x):
    hx = 0.5 * x
    return hx * (1.0 + jnp.tanh(hx))


def _softplus(x):
    return jnp.maximum(x, 0.0) + jnp.log1p(jnp.exp(-jnp.abs(x)))


def _split3(x):
    hi = x.astype(BF16)
    r1 = x - hi.astype(F32)
    mid = r1.astype(BF16)
    lo = (r1 - mid.astype(F32)).astype(BF16)
    return hi, mid, lo


def _largest_tile(n, cap):
    t = cap
    while n % t:
        t //= 2
    return t


def _row_specs(x, tm, off=0):
    if not isinstance(x, tuple):
        return [pl.BlockSpec((tm, D_MODEL), lambda i, *_: (i + off, 0))], [x]
    n_first = x[0].shape[0] // tm
    return [pl.BlockSpec((tm, D_MODEL), lambda i, *_: (jnp.minimum(i + off, n_first - 1), 0)),
            pl.BlockSpec((tm, D_MODEL), lambda i, *_: (jnp.maximum(i + off - n_first, 0), 0))], list(x)


def _inproj_body(*refs, tm, n_first):
    x_refs = refs[:-4]
    g_ref, w_ref, o_ref, h_ref = refs[-4:]
    i = pl.program_id(0)

    def normalise(x_ref):
        rows = 128
        def norm(r, c):
            sl = pl.ds(pl.multiple_of(r * rows, rows), rows)
            x = x_ref[sl, :]
            ms = jnp.mean(x * x, axis=-1, keepdims=True)
            h_ref[sl, :] = (x * lax.rsqrt(ms + EPS) * g_ref[...]).astype(BF16)
            return c
        lax.fori_loop(0, tm // rows, norm, 0)

    first_col = pl.program_id(1) == 0
    if len(x_refs) == 1:
        pl.when(first_col)(lambda: normalise(x_refs[0]))
    else:
        pl.when(jnp.logical_and(first_col, i < n_first))(lambda: normalise(x_refs[0]))
        pl.when(jnp.logical_and(first_col, i >= n_first))(lambda: normalise(x_refs[1]))

    o_ref[...] = jnp.dot(h_ref[...], w_ref[...], preferred_element_type=F32).astype(o_ref.dtype)


def _inproj(x, t, gain, w_in, layer, tm, tn):
    x_specs, x_args = _row_specs(x, tm)
    n_first = x_args[0].shape[0] // tm
    return pl.pallas_call(
        functools.partial(_inproj_body, tm=tm, n_first=n_first),
        grid=(t // tm, 2 * D_MODEL // tn),
        in_specs=x_specs + [
            pl.BlockSpec((None, 1, D_MODEL), lambda i, j: (layer, 0, 0)),
            pl.BlockSpec((None, D_MODEL, tn), lambda i, j: (layer, 0, j)),
        ],
        out_specs=[pl.BlockSpec((tm, tn), lambda i, j: (i, j)),
                   pl.BlockSpec((tm, D_MODEL), lambda i, j: (i, 0))],
        out_shape=[jax.ShapeDtypeStruct((t, 2 * D_MODEL), BF16),
                   jax.ShapeDtypeStruct((t, D_MODEL), BF16)],
        compiler_params=pltpu.CompilerParams(
            dimension_semantics=("arbitrary", "arbitrary"), vmem_limit_bytes=VMEM_LIMIT),
        name="inproj",
    )(*x_args, gain, w_in)


def _rglru_body(*refs, reverse, n_extra, n_a, n_t, tb, cb, rc, unroll):
    refs = list(refs)
    x_ref, xp_ref, xn_ref = refs[:3]
    del refs[:3]
    if reverse:
        gr_ref, hf_ref = refs[:2]
        del refs[:2]
    cw_ref, cbias_ref, w_ref, ba_ref, bx_ref, lam_ref, hn_ref = refs[:7]
    wx_refs = refs[7:7 + n_extra]
    o_ref = refs[7 + n_extra]
    px_refs = refs[8 + n_extra:8 + 2 * n_extra]
    xs_ref, a_s, h_s, carry_ref, wb_ref = refs[8 + 2 * n_extra:]

    nh = cb // HEAD
    nchunk = tb // rc
    nv = rc // 8
    t = pl.program_id(1)
    tix = (n_t - 1 - t) if reverse else t
    seq_first = jnp.logical_or(tix == 0, tix == n_a)
    seq_last = jnp.logical_or(tix == n_a - 1, tix == n_t - 1)
    start = seq_last if reverse else seq_first

    @pl.when(start)
    def _():
        carry_ref[...] = jnp.zeros_like(carry_ref)

    @pl.when(t == 0)
    def _():
        for e, wx_ref in enumerate(wx_refs):
            wb_ref[e] = wx_ref[...].astype(BF16)

    pm = jnp.where(seq_first, 0.0, 1.0).astype(F32)
    nm = jnp.where(seq_last, 0.0, 1.0).astype(F32)
    xs_ref[0:8, :] = xp_ref[...].astype(F32)[HALO - 8:HALO, :] * pm
    xs_ref[8:8 + tb, :] = x_ref[...].astype(F32)
    xs_ref[8 + tb:16 + tb, :] = xn_ref[...].astype(F32)[0:8, :] * nm

    neg_c_sp = (-RG_C * LOG2E) * _softplus(-lam_ref[...])
    sub = lax.broadcasted_iota(jnp.int32, (nv, 8, HEAD), 1)

    def chunk(k, c):
        kk = (nchunk - 1 - k) if reverse else k
        r0 = pl.multiple_of(kk * rc, rc)
        rows = pl.ds(r0, rc)
        hn = hn_ref[rows, :]
        extra = list(enumerate(px_refs))

        def project(pairs):
            for e, px_ref in pairs:
                px_ref[rows, :] = jnp.dot(hn, wb_ref[e], preferred_element_type=F32).astype(px_ref.dtype)

        for h in range(nh):
            project(extra[h::nh + 1])
            hs = slice(h * HEAD, (h + 1) * HEAD)
            xw = xs_ref[pl.ds(r0, rc + 16), hs].reshape(nv + 2, 8, HEAD)

            def tap(s):
                if s == 0:
                    return xw[1:nv + 1]
                rt = pltpu.roll(xw, s % 8, axis=1)
                if s > 0:
                    return jnp.where(sub < s, rt[0:nv], rt[1:nv + 1])
                return jnp.where(sub < 8 + s, rt[1:nv + 1], rt[2:nv + 2])

            xc = (cbias_ref[:, hs] + tap(2) * cw_ref[0:1, hs] + tap(1) * cw_ref[1:2, hs]
                  + tap(0) * cw_ref[2:3, hs] + tap(-1) * cw_ref[3:4, hs]).reshape(rc, HEAD)
            g = jnp.dot(xc.astype(BF16), w_ref[h], preferred_element_type=F32)
            r = _sigmoid(g[:, :HEAD] + ba_ref[:, hs])
            i = _sigmoid(g[:, HEAD:] + bx_ref[:, hs])
            a = jnp.exp2(r * neg_c_sp[:, hs])
            bv = jnp.sqrt(1.0 - a * a) * (i * xc)
            a = a.reshape(nv, 8, HEAD)
            bv = bv.reshape(nv, 8, HEAD)
            for d in (1, 2, 4):
                a_sh = pltpu.roll(a, (8 - d) if reverse else d, axis=1)
                b_sh = pltpu.roll(bv, (8 - d) if reverse else d, axis=1)
                valid = (sub < 8 - d) if reverse else (sub >= d)
                bv = jnp.where(valid, a * b_sh + bv, bv)
                a = jnp.where(valid, a * a_sh, a)
            a_s[h] = a.reshape(rc, HEAD)
            h_s[h] = bv.reshape(rc, HEAD)

        project(extra[nh::nh + 1])

        def step(i, cs):
            ii = (nv - 1 - i) if reverse else i
            sl = pl.ds(pl.multiple_of(ii * 8, 8), 8)
            out = []
            for h in range(nh):
                hv = h_s[h, sl, :] + a_s[h, sl, :] * cs[h]
                h_s[h, sl, :] = hv
                out.append(hv[0:1, :] if reverse else hv[7:8, :])
            return tuple(out)

        cs = tuple(carry_ref[:, h * HEAD:(h + 1) * HEAD] for h in range(nh))
        cs = lax.fori_loop(0, nv, step, cs, unroll=True)
        for h in range(nh):
            hs = slice(h * HEAD, (h + 1) * HEAD)
            carry_ref[:, hs] = cs[h]
            if reverse:
                gate = gr_ref[rows, hs].astype(F32)
                o_ref[rows, hs] = ((hf_ref[rows, hs] + h_s[h]) * _silu(gate)).astype(o_ref.dtype)
            else:
                o_ref[rows, hs] = h_s[h]

        return c

    lax.fori_loop(0, nchunk, chunk, 0, unroll=unroll)


def _rglru(proj, hf, hn, w_in, extra_cols, conv_w, conv_b, w_cat, ba, bx, lam, layer, n_a, reverse,
           tb, cb, rc, unroll):
    t = proj.shape[0]
    n_t = t // tb
    ncg = D_MODEL // cb
    hb = tb // HALO
    nhalo = t // HALO
    d = 1 if reverse else 0
    n_extra = len(extra_cols)

    def tmap(ti):
        return (n_t - 1 - ti) if reverse else ti

    blk = lambda col: pl.BlockSpec((tb, cb), lambda c, ti: (tmap(ti), col * ncg + c))
    vec = lambda: pl.BlockSpec((None, 1, cb), lambda c, ti: (layer, 0, c))
    dvec = lambda: pl.BlockSpec((None, None, 1, cb), lambda c, ti: (layer, d, 0, c))
    in_specs = [
        blk(COL_XR),
        pl.BlockSpec((HALO, cb), lambda c, ti: (jnp.maximum(tmap(ti) * hb - 1, 0), c)),
        pl.BlockSpec((HALO, cb), lambda c, ti: (jnp.minimum((tmap(ti) + 1) * hb, nhalo - 1), c)),
    ]
    args = [proj, proj, proj]
    if reverse:
        in_specs += [blk(COL_GR), pl.BlockSpec((tb, cb), lambda c, ti: (tmap(ti), c))]
        args += [proj, hf]
    in_specs += [
        pl.BlockSpec((None, 4, cb), lambda c, ti: (layer, 0, c)),
        vec(),
        pl.BlockSpec((None, None, cb // HEAD, HEAD, 2 * HEAD), lambda c, ti: (layer, d, c, 0, 0)),
        dvec(), dvec(), dvec(),
        pl.BlockSpec((tb, D_MODEL), lambda c, ti: (tmap(ti), 0)),
    ]
    args += [conv_w, conv_b, w_cat, ba, bx, lam, hn]
    for col in extra_cols:
        in_specs.append(pl.BlockSpec((None, D_MODEL, cb), lambda c, ti, col=col: (layer, 0, col * ncg + c),
                                     pipeline_mode=pl.Buffered(1)))
        args.append(w_in)
    out_blk = lambda: pl.BlockSpec((tb, cb), lambda c, ti: (tmap(ti), c))
    nh = cb // HEAD
    return pl.pallas_call(
        functools.partial(_rglru_body, reverse=reverse, n_extra=n_extra, n_a=n_a, n_t=n_t,
                          tb=tb, cb=cb, rc=rc, unroll=unroll),
        grid=(ncg, n_t),
        in_specs=in_specs,
        out_specs=[out_blk() for _ in range(1 + n_extra)],
        out_shape=[jax.ShapeDtypeStruct((t, D_MODEL), BF16 if reverse else F32)]
        + [jax.ShapeDtypeStruct((t, D_MODEL), BF16) for _ in range(n_extra)],
        scratch_shapes=[
            pltpu.VMEM((tb + 16, cb), F32),
            pltpu.VMEM((nh, rc, HEAD), F32),
            pltpu.VMEM((nh, rc, HEAD), F32),
            pltpu.VMEM((1, cb), F32),
            pltpu.VMEM((n_extra, D_MODEL, cb), BF16),
        ],
        compiler_params=pltpu.CompilerParams(
            dimension_semantics=("arbitrary", "arbitrary"), vmem_limit_bytes=VMEM_LIMIT),
        name="rglru_bwd" if reverse else "rglru_fwd",
    )(*args)


def _decay_sum_matrix(reverse):
    t = np.arange(CHUNK)[:, None]
    u = np.arange(CHUNK)[None, :]
    if reverse:
        blocks = [u >= t, u < t]
    else:
        blocks = [u <= t, u > t]
    for m in LEVELS:
        upper = (t & m) != 0
        if reverse:
            rho = (t // (2 * m)) * (2 * m) + m
            blocks.append(np.where(upper, (u >= rho) & (u < t), (u >= t) & (u < rho)))
        else:
            rho = (t // (2 * m)) * (2 * m) + m - 1
            blocks.append(np.where(upper, (u > rho) & (u <= t), (u > t) & (u <= rho)))
    mat = np.concatenate(blocks, axis=0).astype(np.float32)
    return np.concatenate([mat, mat, mat], axis=1)


def _hgrn_body(*refs, reverse, layer, n_a, n_t, tb, cb, group, unroll):
    if reverse:
        (q_ref, z_ref, v_ref, g_ref, of_ref, lower_ref, gain_ref, m_ref, o_ref, st_ref) = refs
    else:
        (q_ref, z_ref, v_ref, lower_ref, m_ref, o_ref, st_ref) = refs
    nh = cb // HEAD
    nchunk = tb // CHUNK
    t = pl.program_id(1)
    tix = (n_t - 1 - t) if reverse else t
    if reverse:
        start = jnp.logical_or(tix == n_a - 1, tix == n_t - 1)
    else:
        start = jnp.logical_or(tix == 0, tix == n_a)

    @pl.when(start)
    def _():
        st_ref[...] = jnp.zeros_like(st_ref)

    low = lower_ref[...]
    e = jnp.exp(low - jnp.max(low, axis=0, keepdims=True))
    den = jnp.sum(e, axis=0, keepdims=True)
    if layer == 0:
        lb = jnp.zeros_like(den)
    else:
        lb = jnp.sum(e[1:layer + 1, :], axis=0, keepdims=True) / den
    log_lb = jnp.log(lb)
    log1m_lb = jnp.log1p(-lb)
    half_1m_lb = 0.5 * (1.0 - lb)

    ti = lax.broadcasted_iota(jnp.int32, (CHUNK, CHUNK), 0)
    si = lax.broadcasted_iota(jnp.int32, (CHUNK, CHUNK), 1)
    txs = ti ^ si
    causal = (ti <= si) if reverse else (ti >= si)
    rowi = lax.broadcasted_iota(jnp.int32, (CHUNK, HEAD), 0)
    q_side = [((rowi & m) == 0) if reverse else ((rowi & m) != 0) for m in LEVELS]

    hsl = [slice(h * HEAD, (h + 1) * HEAD) for h in range(nh)]
    items = [(j, h) for j in range(group) for h in range(nh)]

    def trip_rows(k):
        rows = []
        for j in range(group):
            kc = k * group + j
            kk = (nchunk - 1 - kc) if reverse else kc
            rows.append(pl.ds(kk * CHUNK, CHUNK))
        return rows

    def gates(rows):
        qk = {}
        rhs = {}
        for j in range(group):
            for pair in range(nh // 2):
                parts = []
                for h in (2 * pair, 2 * pair + 1):
                    hs = hsl[h]
                    z = z_ref[rows[j], hs].astype(F32)
                    q = _silu(q_ref[rows[j], hs].astype(F32))
                    ez = jnp.exp(-jnp.abs(z))
                    log_sig = jnp.minimum(z, 0.0) - jnp.log(1.0 + ez)
                    if layer == 0:
                        logf = log_sig
                    else:
                        u = log1m_lb[:, hs] + log_sig
                        la = log_lb[:, hs]
                        logf = jnp.maximum(la, u) + jnp.log(1.0 + jnp.exp(-jnp.abs(la - u)))
                    key = half_1m_lb[:, hs] - half_1m_lb[:, hs] * jnp.tanh(0.5 * z)
                    qk[(j, h)] = (q, key)
                    parts.append(_split3(logf * LOG2E))
                rhs[(j, pair)] = jnp.concatenate(
                    [jnp.concatenate([parts[0][i], parts[1][i]], axis=1) for i in range(3)], axis=0)
        return qk, rhs

    def decay_sums(rhs):
        dall = {}
        for (j, pair), r in rhs.items():
            dpair = jnp.dot(m_ref[...], r, preferred_element_type=F32)
            dall[(j, 2 * pair)] = dpair[:, :HEAD]
            dall[(j, 2 * pair + 1)] = dpair[:, HEAD:]
        return dall

    def score_free(rows, qk, dall):
        levels = {}
        o_inter = {}
        for j, h in items:
            q, key = qk[(j, h)]
            d = dall[(j, h)]
            q16 = q.astype(BF16)
            k16 = key.astype(BF16)
            p = [lax.dot_general(q16, k16, _NT, preferred_element_type=F32)]
            for li in range(len(LEVELS)):
                d_l = d[(2 + li) * CHUNK:(3 + li) * CHUNK, :]
                w_l = jnp.where(q_side[li], q16, k16) * jnp.exp2(d_l.astype(BF16))
                p.append(lax.dot_general(w_l, w_l, _NT, preferred_element_type=F32))
            levels[(j, h)] = p

            b2 = d[0:CHUNK, :]
            st = st_ref[h]
            q_in = (q * jnp.exp2(b2)).astype(BF16)
            o_inter[(j, h)] = lax.dot_general(q_in, st.astype(BF16), _NT, preferred_element_type=F32)
            kd = (key * jnp.exp2(d[CHUNK:2 * CHUNK, :])).astype(BF16)
            total = b2[0:1, :] if reverse else b2[CHUNK - 1:CHUNK, :]
            st_ref[h] = st * jnp.exp2(total) + lax.dot_general(
                v_ref[rows[j], hsl[h]], kd, _TN, preferred_element_type=F32)
        return levels, o_inter

    def finish(rows, levels, o_inter):
        for j, h in items:
            hs = hsl[h]
            scores = levels[(j, h)][0].astype(BF16)
            for li, m in enumerate(LEVELS):
                scores = jnp.where(txs >= m, levels[(j, h)][li + 1].astype(BF16), scores)
            scores = jnp.where(causal, scores, jnp.zeros_like(scores))
            o = o_inter[(j, h)] + jnp.dot(scores, v_ref[rows[j], hs], preferred_element_type=F32)
            if reverse:
                ot = of_ref[rows[j], hs] + o
                ms = jnp.mean(ot * ot, axis=-1, keepdims=True)
                y = ot * lax.rsqrt(ms + EPS) * gain_ref[:, hs]
                o_ref[rows[j], hs] = (y * _silu(g_ref[rows[j], hs].astype(F32))).astype(o_ref.dtype)
            else:
                o_ref[rows[j], hs] = o

    n_trips = nchunk // group
    rows = trip_rows(0)
    qk, rhs = gates(rows)
    dall = decay_sums(rhs)
    for k in range(n_trips):
        levels, o_inter = score_free(rows, qk, dall)
        if k + 1 < n_trips:
            next_rows = trip_rows(k + 1)
            qk, rhs = gates(next_rows)
            dall = decay_sums(rhs)
        finish(rows, levels, o_inter)
        if k + 1 < n_trips:
            rows = next_rows


def _hgrn(q, z, v, gate, of, hg_lower, hg_gain, layer, n_a, reverse, tb, cb, group, unroll):
    t = q.shape[0]
    n_t = t // tb
    depth = hg_lower.shape[1]
    d = 1 if reverse else 0
    msum = jnp.asarray(_decay_sum_matrix(reverse), dtype=BF16)

    def tmap(ti):
        return (n_t - 1 - ti) if reverse else ti

    blk = lambda: pl.BlockSpec((tb, cb), lambda c, ti: (tmap(ti), c))
    in_specs = [blk(), blk(), blk()]
    args = [q, z, v]
    if reverse:
        in_specs += [blk(), blk()]
        args += [gate, of]
    in_specs.append(pl.BlockSpec((None, depth, cb), lambda c, ti: (d, 0, c)))
    args.append(hg_lower)
    if reverse:
        in_specs.append(pl.BlockSpec((None, 1, cb), lambda c, ti: (layer, 0, c)))
        args.append(hg_gain)
    in_specs.append(pl.BlockSpec(msum.shape, lambda c, ti: (0, 0)))
    args.append(msum)
    return pl.pallas_call(
        functools.partial(_hgrn_body, reverse=reverse, layer=layer, n_a=n_a, n_t=n_t, tb=tb, cb=cb,
                          group=group, unroll=unroll),
        grid=(D_MODEL // cb, n_t),
        in_specs=in_specs,
        out_specs=blk(),
        out_shape=jax.ShapeDtypeStruct((t, D_MODEL), BF16 if reverse else F32),
        scratch_shapes=[pltpu.VMEM((cb // HEAD, HEAD, HEAD), F32)],
        compiler_params=pltpu.CompilerParams(
            dimension_semantics=("arbitrary", "arbitrary"), vmem_limit_bytes=VMEM_LIMIT),
        name="hgrn_bwd" if reverse else "hgrn_fwd",
    )(*args)


def _merge_body(*refs, final, off, n_first):
    x_refs = refs[:-9]
    ar_ref, ah_ref, mr_ref, mh_ref, wr_ref, wh_ref, wo_ref, fg_ref, o_ref = refs[-9:]
    if len(x_refs) == 1:
        x = x_refs[0][...]
    else:
        x = jnp.where(pl.program_id(0) + off < n_first, x_refs[0][...], x_refs[1][...])
    yr = jnp.dot(ar_ref[...], wr_ref[...], preferred_element_type=F32)
    yh = jnp.dot(ah_ref[...], wh_ref[...], preferred_element_type=F32)
    merged = _sigmoid(mr_ref[...].astype(F32)) * yr + _sigmoid(mh_ref[...].astype(F32)) * yh
    x = x + jnp.dot(merged.astype(BF16), wo_ref[...], preferred_element_type=F32)
    if final:
        ms = jnp.mean(x * x, axis=-1, keepdims=True)
        x = x * lax.rsqrt(ms + EPS) * fg_ref[...]
    o_ref[...] = x


def _merge(x, a_r, a_h, m_r, m_h, w_dr, w_dh, w_o, final_gain, layer, final, tm, row0, n_rows):
    off = row0 // tm
    x_specs, x_args = _row_specs(x, tm, off)
    n_first = x_args[0].shape[0] // tm
    row = lambda: pl.BlockSpec((tm, D_MODEL), lambda i: (i + off, 0))
    wgt = lambda: pl.BlockSpec((None, D_MODEL, D_MODEL), lambda i: (layer, 0, 0),
                               pipeline_mode=pl.Buffered(1))
    return pl.pallas_call(
        functools.partial(_merge_body, final=final, off=off, n_first=n_first),
        grid=(n_rows // tm,),
        in_specs=x_specs + [row(), row(), row(), row(), wgt(), wgt(), wgt(),
                            pl.BlockSpec((1, D_MODEL), lambda i: (0, 0))],
        out_specs=pl.BlockSpec((tm, D_MODEL), lambda i: (i, 0)),
        out_shape=jax.ShapeDtypeStruct((n_rows, D_MODEL), F32),
        compiler_params=pltpu.CompilerParams(
            dimension_semantics=("arbitrary",), vmem_limit_bytes=VMEM_LIMIT),
        name="merge",
    )(*x_args, a_r, a_h, m_r, m_h, w_dr, w_dh, w_o, final_gain)


def kernel(x_prompt, x_sample, norm_gain, w_in, conv_w, conv_b, rg_wa, rg_ba, rg_wx, rg_bx, rg_lambda,
           hg_lower, hg_norm_gain, w_down_r, w_down_h, w_out, final_gain):
    depth = w_in.shape[0]
    s_a = x_prompt.shape[0] * x_prompt.shape[1]
    s_b = x_sample.shape[0] * x_sample.shape[1]
    assert x_prompt.shape[0] == 1 and x_sample.shape[0] == 1
    t = s_a + s_b
    x = (x_prompt.reshape(s_a, D_MODEL), x_sample.reshape(s_b, D_MODEL))

    tb = _largest_tile(min(s_a, s_b), 1024)
    n_a = s_a // tb
    tm_in = _largest_tile(min(s_a, s_b), 512)
    tm_merge = _largest_tile(min(s_a, s_b), 256)
    rg_cb = 512
    rg_rc = 128
    rg_unroll = min(4, tb // rg_rc)
    hg_group = 4
    hg_unroll = min(4, tb // (CHUNK * hg_group))

    w_rg_b = w_in[:, :, :2 * D_MODEL].astype(BF16)
    w_dr_b = w_down_r.astype(BF16)
    w_dh_b = w_down_h.astype(BF16)
    w_o_b = w_out.astype(BF16)
    w_cat = jnp.concatenate([rg_wa, rg_wx], axis=-1).astype(BF16)
    gain3 = norm_gain.reshape(depth, 1, D_MODEL)
    conv_b3 = conv_b.reshape(depth, 1, D_MODEL)
    ba4 = rg_ba.reshape(depth, 2, 1, D_MODEL)
    bx4 = rg_bx.reshape(depth, 2, 1, D_MODEL)
    lam4 = rg_lambda.reshape(depth, 2, 1, D_MODEL)
    hgain3 = hg_norm_gain.reshape(depth, 1, D_MODEL)
    fgain = final_gain.reshape(1, D_MODEL)

    for l in range(depth):
        proj, hn = _inproj(x, t, gain3, w_rg_b, l, tm_in if l == 0 else _largest_tile(t, 1024), 1024)
        rg = (conv_w, conv_b3, w_cat, ba4, bx4, lam4, l, n_a)
        hf, q, zf, v = _rglru(proj, None, hn, w_in, (COL_Q, COL_ZF, COL_V), *rg, False,
                              tb, rg_cb // 2, rg_rc, min(2 * rg_unroll, tb // rg_rc))
        a_r, zb, gh, m_r, m_h = _rglru(proj, hf, hn, w_in, (COL_ZB, COL_GH, COL_MR, COL_MH), *rg, True,
                                       tb, rg_cb, rg_rc, rg_unroll)
        of = _hgrn(q, zf, v, None, None, hg_lower, hgain3, l, n_a, False, tb, 512, hg_group, hg_unroll)
        a_h = _hgrn(q, zb, v, gh, of, hg_lower, hgain3, l, n_a, True, tb, 512, hg_group, hg_unroll)
        mg = (a_r, a_h, m_r, m_h, w_dr_b, w_dh_b, w_o_b, fgain, l)
        if l < depth - 1:
            x = _merge(x, *mg, False, tm_merge, 0, t)
        else:
            y_a = _merge(x, *mg, True, tm_merge, 0, s_a)
            y_b = _merge(x, *mg, True, tm_merge, s_a, s_b)
    return (y_a.reshape(x_prompt.shape), y_b.reshape(x_sample.shape))
```

```python
import functools

import numpy as np
import jax
import jax.numpy as jnp
from jax import lax
from jax.experimental import pallas as pl
from jax.experimental.pallas import tpu as pltpu

F32 = jnp.float32
BF16 = jnp.bfloat16

D_MODEL = 2048
HEAD = 128
N_SPLITS = 9
COL_XR, COL_GR, COL_Q, COL_ZF, COL_ZB, COL_V, COL_GH, COL_MR, COL_MH = range(N_SPLITS)
EPS = 1e-6
RG_C = 8.0
CHUNK = 64
LEVELS = (1, 2, 4, 8, 16, 32)
LOG2E = 1.4426950408889634
HALO = 16
VMEM_LIMIT = 56 * 1024 * 1024

_NT = (((1,), (1,)), ((), ()))
_TN = (((0,), (0,)), ((), ()))


def _sigmoid(x):
    return 0.5 * jnp.tanh(0.5 * x) + 0.5


def _silu(x):
    hx = 0.5 * x
    return hx * (1.0 + jnp.tanh(hx))


def _softplus(x):
    return jnp.maximum(x, 0.0) + jnp.log1p(jnp.exp(-jnp.abs(x)))


def _split3(x):
    hi = x.astype(BF16)
    r1 = x - hi.astype(F32)
    mid = r1.astype(BF16)
    lo = (r1 - mid.astype(F32)).astype(BF16)
    return hi, mid, lo


def _largest_tile(n, cap):
    t = cap
    while n % t:
        t //= 2
    return t


def _row_specs(x, tm, off=0):
    if not isinstance(x, tuple):
        return [pl.BlockSpec((tm, D_MODEL), lambda i, *_: (i + off, 0))], [x]
    n_first = x[0].shape[0] // tm
    return [pl.BlockSpec((tm, D_MODEL), lambda i, *_: (jnp.minimum(i + off, n_first - 1), 0)),
            pl.BlockSpec((tm, D_MODEL), lambda i, *_: (jnp.maximum(i + off - n_first, 0), 0))], list(x)


def _inproj_body(*refs, tm, n_first):
    x_refs = refs[:-4]
    g_ref, w_ref, o_ref, h_ref = refs[-4:]
    i = pl.program_id(0)

    def normalise(x_ref):
        rows = 128
        def norm(r, c):
            sl = pl.ds(pl.multiple_of(r * rows, rows), rows)
            x = x_ref[sl, :]
            ms = jnp.mean(x * x, axis=-1, keepdims=True)
            h_ref[sl, :] = (x * lax.rsqrt(ms + EPS) * g_ref[...]).astype(BF16)
            return c
        lax.fori_loop(0, tm // rows, norm, 0)

    first_col = pl.program_id(1) == 0
    if len(x_refs) == 1:
        pl.when(first_col)(lambda: normalise(x_refs[0]))
    else:
        pl.when(jnp.logical_and(first_col, i < n_first))(lambda: normalise(x_refs[0]))
        pl.when(jnp.logical_and(first_col, i >= n_first))(lambda: normalise(x_refs[1]))

    o_ref[...] = jnp.dot(h_ref[...], w_ref[...], preferred_element_type=F32).astype(o_ref.dtype)


def _inproj(x, t, gain, w_in, layer, tm, tn):
    x_specs, x_args = _row_specs(x, tm)
    n_first = x_args[0].shape[0] // tm
    return pl.pallas_call(
        functools.partial(_inproj_body, tm=tm, n_first=n_first),
        grid=(t // tm, 2 * D_MODEL // tn),
        in_specs=x_specs + [
            pl.BlockSpec((None, 1, D_MODEL), lambda i, j: (layer, 0, 0)),
            pl.BlockSpec((None, D_MODEL, tn), lambda i, j: (layer, 0, j)),
        ],
        out_specs=[pl.BlockSpec((tm, tn), lambda i, j: (i, j)),
                   pl.BlockSpec((tm, D_MODEL), lambda i, j: (i, 0))],
        out_shape=[jax.ShapeDtypeStruct((t, 2 * D_MODEL), BF16),
                   jax.ShapeDtypeStruct((t, D_MODEL), BF16)],
        compiler_params=pltpu.CompilerParams(
            dimension_semantics=("arbitrary", "arbitrary"), vmem_limit_bytes=VMEM_LIMIT),
        name="inproj",
    )(*x_args, gain, w_in)


def _rglru_body(*refs, reverse, n_extra, n_a, n_t, tb, cb, rc, unroll):
    refs = list(refs)
    x_ref, xp_ref, xn_ref = refs[:3]
    del refs[:3]
    if reverse:
        gr_ref, hf_ref = refs[:2]
        del refs[:2]
    cw_ref, cbias_ref, w_ref, ba_ref, bx_ref, lam_ref, hn_ref = refs[:7]
    wx_refs = refs[7:7 + n_extra]
    o_ref = refs[7 + n_extra]
    px_refs = refs[8 + n_extra:8 + 2 * n_extra]
    xs_ref, a_s, h_s, carry_ref, wb_ref = refs[8 + 2 * n_extra:]

    nh = cb // HEAD
    nchunk = tb // rc
    nv = rc // 8
    t = pl.program_id(1)
    tix = (n_t - 1 - t) if reverse else t
    seq_first = jnp.logical_or(tix == 0, tix == n_a)
    seq_last = jnp.logical_or(tix == n_a - 1, tix == n_t - 1)
    start = seq_last if reverse else seq_first

    @pl.when(start)
    def _():
        carry_ref[...] = jnp.zeros_like(carry_ref)

    @pl.when(t == 0)
    def _():
        for e, wx_ref in enumerate(wx_refs):
            wb_ref[e] = wx_ref[...].astype(BF16)

    pm = jnp.where(seq_first, 0.0, 1.0).astype(F32)
    nm = jnp.where(seq_last, 0.0, 1.0).astype(F32)
    xs_ref[0:8, :] = xp_ref[...].astype(F32)[HALO - 8:HALO, :] * pm
    xs_ref[8:8 + tb, :] = x_ref[...].astype(F32)
    xs_ref[8 + tb:16 + tb, :] = xn_ref[...].astype(F32)[0:8, :] * nm

    neg_c_sp = (-RG_C * LOG2E) * _softplus(-lam_ref[...])
    sub = lax.broadcasted_iota(jnp.int32, (nv, 8, HEAD), 1)

    def chunk(k, c):
        kk = (nchunk - 1 - k) if reverse else k
        r0 = pl.multiple_of(kk * rc, rc)
        rows = pl.ds(r0, rc)
        hn = hn_ref[rows, :]
        extra = list(enumerate(px_refs))

        def project(pairs):
            for e, px_ref in pairs:
                px_ref[rows, :] = jnp.dot(hn, wb_ref[e], preferred_element_type=F32).astype(px_ref.dtype)

        for h in range(nh):
            project(extra[h::nh + 1])
            hs = slice(h * HEAD, (h + 1) * HEAD)
            xw = xs_ref[pl.ds(r0, rc + 16), hs].reshape(nv + 2, 8, HEAD)

            def tap(s):
                if s == 0:
                    return xw[1:nv + 1]
                rt = pltpu.roll(xw, s % 8, axis=1)
                if s > 0:
                    return jnp.where(sub < s, rt[0:nv], rt[1:nv + 1])
                return jnp.where(sub < 8 + s, rt[1:nv + 1], rt[2:nv + 2])

            xc = (cbias_ref[:, hs] + tap(2) * cw_ref[0:1, hs] + tap(1) * cw_ref[1:2, hs]
                  + tap(0) * cw_ref[2:3, hs] + tap(-1) * cw_ref[3:4, hs]).reshape(rc, HEAD)
            g = jnp.dot(xc.astype(BF16), w_ref[h], preferred_element_type=F32)
            r = _sigmoid(g[:, :HEAD] + ba_ref[:, hs])
            i = _sigmoid(g[:, HEAD:] + bx_ref[:, hs])
            a = jnp.exp2(r * neg_c_sp[:, hs])
            bv = jnp.sqrt(1.0 - a * a) * (i * xc)
            a = a.reshape(nv, 8, HEAD)
            bv = bv.reshape(nv, 8, HEAD)
            for d in (1, 2, 4):
                a_sh = pltpu.roll(a, (8 - d) if reverse else d, axis=1)
                b_sh = pltpu.roll(bv, (8 - d) if reverse else d, axis=1)
                valid = (sub < 8 - d) if reverse else (sub >= d)
                bv = jnp.where(valid, a * b_sh + bv, bv)
                a = jnp.where(valid, a * a_sh, a)
            a_s[h] = a.reshape(rc, HEAD)
            h_s[h] = bv.reshape(rc, HEAD)

        project(extra[nh::nh + 1])

        def step(i, cs):
            ii = (nv - 1 - i) if reverse else i
            sl = pl.ds(pl.multiple_of(ii * 8, 8), 8)
            out = []
            for h in range(nh):
                hv = h_s[h, sl, :] + a_s[h, sl, :] * cs[h]
                h_s[h, sl, :] = hv
                out.append(hv[0:1, :] if reverse else hv[7:8, :])
            return tuple(out)

        cs = tuple(carry_ref[:, h * HEAD:(h + 1) * HEAD] for h in range(nh))
        cs = lax.fori_loop(0, nv, step, cs, unroll=True)
        for h in range(nh):
            hs = slice(h * HEAD, (h + 1) * HEAD)
            carry_ref[:, hs] = cs[h]
            if reverse:
                gate = gr_ref[rows, hs].astype(F32)
                o_ref[rows, hs] = ((hf_ref[rows, hs] + h_s[h]) * _silu(gate)).astype(o_ref.dtype)
            else:
                o_ref[rows, hs] = h_s[h]

        return c

    lax.fori_loop(0, nchunk, chunk, 0, unroll=unroll)


def _rglru(proj, hf, hn, w_in, extra_cols, conv_w, conv_b, w_cat, ba, bx, lam, layer, n_a, reverse,
           tb, cb, rc, unroll):
    t = proj.shape[0]
    n_t = t // tb
    ncg = D_MODEL // cb
    hb = tb // HALO
    nhalo = t // HALO
    d = 1 if reverse else 0
    n_extra = len(extra_cols)

    def tmap(ti):
        return (n_t - 1 - ti) if reverse else ti

    blk = lambda col: pl.BlockSpec((tb, cb), lambda c, ti: (tmap(ti), col * ncg + c))
    vec = lambda: pl.BlockSpec((None, 1, cb), lambda c, ti: (layer, 0, c))
    dvec = lambda: pl.BlockSpec((None, None, 1, cb), lambda c, ti: (layer, d, 0, c))
    in_specs = [
        blk(COL_XR),
        pl.BlockSpec((HALO, cb), lambda c, ti: (jnp.maximum(tmap(ti) * hb - 1, 0), c)),
        pl.BlockSpec((HALO, cb), lambda c, ti: (jnp.minimum((tmap(ti) + 1) * hb, nhalo - 1), c)),
    ]
    args = [proj, proj, proj]
    if reverse:
        in_specs += [blk(COL_GR), pl.BlockSpec((tb, cb), lambda c, ti: (tmap(ti), c))]
        args += [proj, hf]
    in_specs += [
        pl.BlockSpec((None, 4, cb), lambda c, ti: (layer, 0, c)),
        vec(),
        pl.BlockSpec((None, None, cb // HEAD, HEAD, 2 * HEAD), lambda c, ti: (layer, d, c, 0, 0)),
        dvec(), dvec(), dvec(),
        pl.BlockSpec((tb, D_MODEL), lambda c, ti: (tmap(ti), 0)),
    ]
    args += [conv_w, conv_b, w_cat, ba, bx, lam, hn]
    for col in extra_cols:
        in_specs.append(pl.BlockSpec((None, D_MODEL, cb), lambda c, ti, col=col: (layer, 0, col * ncg + c),
                                     pipeline_mode=pl.Buffered(1)))
        args.append(w_in)
    out_blk = lambda: pl.BlockSpec((tb, cb), lambda c, ti: (tmap(ti), c))
    nh = cb // HEAD
    return pl.pallas_call(
        functools.partial(_rglru_body, reverse=reverse, n_extra=n_extra, n_a=n_a, n_t=n_t,
                          tb=tb, cb=cb, rc=rc, unroll=unroll),
        grid=(ncg, n_t),
        in_specs=in_specs,
        out_specs=[out_blk() for _ in range(1 + n_extra)],
        out_shape=[jax.ShapeDtypeStruct((t, D_MODEL), BF16 if reverse else F32)]
        + [jax.ShapeDtypeStruct((t, D_MODEL), BF16) for _ in range(n_extra)],
        scratch_shapes=[
            pltpu.VMEM((tb + 16, cb), F32),
            pltpu.VMEM((nh, rc, HEAD), F32),
            pltpu.VMEM((nh, rc, HEAD), F32),
            pltpu.VMEM((1, cb), F32),
            pltpu.VMEM((n_extra, D_MODEL, cb), BF16),
        ],
        compiler_params=pltpu.CompilerParams(
            dimension_semantics=("arbitrary", "arbitrary"), vmem_limit_bytes=VMEM_LIMIT),
        name="rglru_bwd" if reverse else "rglru_fwd",
    )(*args)


def _decay_sum_matrix(reverse):
    t = np.arange(CHUNK)[:, None]
    u = np.arange(CHUNK)[None, :]
    blocks = [u >= t] if reverse else [u <= t]
    for m in LEVELS[1:]:
        upper = (t & m) != 0
        if reverse:
            rho = (t // (2 * m)) * (2 * m) + m
            blocks.append(np.where(upper, (u >= rho) & (u < t), (u >= t) & (u < rho)))
        else:
            rho = (t // (2 * m)) * (2 * m) + m - 1
            blocks.append(np.where(upper, (u > rho) & (u <= t), (u > t) & (u <= rho)))
    mat = np.concatenate(blocks, axis=0).astype(np.float32)
    return np.concatenate([mat, mat, mat], axis=1)


def _hgrn_body(*refs, reverse, layer, n_a, n_t, tb, cb, group, unroll):
    if reverse:
        (q_ref, z_ref, v_ref, g_ref, of_ref, lower_ref, gain_ref, m_ref, o_ref, st_ref) = refs
    else:
        (q_ref, z_ref, v_ref, lower_ref, m_ref, o_ref, st_ref) = refs
    nh = cb // HEAD
    nchunk = tb // CHUNK
    t = pl.program_id(1)
    tix = (n_t - 1 - t) if reverse else t
    if reverse:
        start = jnp.logical_or(tix == n_a - 1, tix == n_t - 1)
    else:
        start = jnp.logical_or(tix == 0, tix == n_a)

    @pl.when(start)
    def _():
        st_ref[...] = jnp.zeros_like(st_ref)

    low = lower_ref[...]
    e = jnp.exp(low - jnp.max(low, axis=0, keepdims=True))
    den = jnp.sum(e, axis=0, keepdims=True)
    if layer == 0:
        lb = jnp.zeros_like(den)
    else:
        lb = jnp.sum(e[1:layer + 1, :], axis=0, keepdims=True) / den
    log_lb = jnp.log(lb)
    log1m_lb = jnp.log1p(-lb)
    half_1m_lb = 0.5 * (1.0 - lb)

    ti = lax.broadcasted_iota(jnp.int32, (CHUNK, CHUNK), 0)
    si = lax.broadcasted_iota(jnp.int32, (CHUNK, CHUNK), 1)
    txs = ti ^ si
    causal = (ti <= si) if reverse else (ti >= si)
    rowi = lax.broadcasted_iota(jnp.int32, (CHUNK, HEAD), 0)
    q_side = [((rowi & m) == 0) if reverse else ((rowi & m) != 0) for m in LEVELS]

    hsl = [slice(h * HEAD, (h + 1) * HEAD) for h in range(nh)]
    items = [(j, h) for j in range(group) for h in range(nh)]

    def trip_rows(k):
        rows = []
        for j in range(group):
            kc = k * group + j
            kk = (nchunk - 1 - kc) if reverse else kc
            rows.append(pl.ds(kk * CHUNK, CHUNK))
        return rows

    def gates(rows):
        qk = {}
        rhs = {}
        for j in range(group):
            for pair in range(nh // 2):
                parts = []
                for h in (2 * pair, 2 * pair + 1):
                    hs = hsl[h]
                    z = z_ref[rows[j], hs].astype(F32)
                    q = _silu(q_ref[rows[j], hs].astype(F32))
                    ez = jnp.exp(-jnp.abs(z))
                    log_sig = jnp.minimum(z, 0.0) - jnp.log(1.0 + ez)
                    if layer == 0:
                        logf = log_sig
                    else:
                        u = log1m_lb[:, hs] + log_sig
                        la = log_lb[:, hs]
                        logf = jnp.maximum(la, u) + jnp.log(1.0 + jnp.exp(-jnp.abs(la - u)))
                    key = half_1m_lb[:, hs] - half_1m_lb[:, hs] * jnp.tanh(0.5 * z)
                    logf2 = logf * LOG2E
                    qk[(j, h)] = (q, key, logf2)
                    parts.append(_split3(logf2))
                rhs[(j, pair)] = jnp.concatenate(
                    [jnp.concatenate([parts[0][i], parts[1][i]], axis=1) for i in range(3)], axis=0)
        return qk, rhs

    def decay_sums(rhs):
        dall = {}
        for (j, pair), r in rhs.items():
            dpair = jnp.dot(m_ref[...], r, preferred_element_type=F32)
            dall[(j, 2 * pair)] = dpair[:, :HEAD]
            dall[(j, 2 * pair + 1)] = dpair[:, HEAD:]
        return dall

    def score_free(rows, qk, dall):
        levels = {}
        o_inter = {}
        for j, h in items:
            q, key, logf2 = qk[(j, h)]
            d = dall[(j, h)]
            q16 = q.astype(BF16)
            k16 = key.astype(BF16)
            p = [lax.dot_general(q16, k16, _NT, preferred_element_type=F32)]
            for li in range(len(LEVELS)):
                if li == 0:
                    d_l = jnp.where(q_side[0], logf2, 0.0)
                else:
                    d_l = d[li * CHUNK:(li + 1) * CHUNK, :]
                w_l = jnp.where(q_side[li], q16, k16) * jnp.exp2(d_l.astype(BF16))
                p.append(lax.dot_general(w_l, w_l, _NT, preferred_element_type=F32))
            levels[(j, h)] = p

            b2 = d[0:CHUNK, :]
            st = st_ref[h]
            o_inter[(j, h)] = ((q * jnp.exp2(b2)).astype(BF16), st.astype(BF16))
            total = b2[0:1, :] if reverse else b2[CHUNK - 1:CHUNK, :]
            kd = (key * jnp.exp2(total - b2)).astype(BF16)
            dec = jnp.transpose(jnp.broadcast_to(jnp.exp2(total), (HEAD, HEAD)))
            st_ref[h] = st * dec + lax.dot_general(
                kd, v_ref[rows[j], hsl[h]], _TN, preferred_element_type=F32)
        return levels, o_inter

    def finish(rows, levels, o_inter):
        for j, h in items:
            hs = hsl[h]
            scores = levels[(j, h)][0].astype(BF16)
            for li, m in enumerate(LEVELS):
                scores = jnp.where(txs >= m, levels[(j, h)][li + 1].astype(BF16), scores)
            scores = jnp.where(causal, scores, jnp.zeros_like(scores))
            q_in, st16 = o_inter[(j, h)]
            o = jnp.dot(jnp.concatenate([q_in, scores], axis=1),
                        jnp.concatenate([st16, v_ref[rows[j], hs]], axis=0), preferred_element_type=F32)
            if reverse:
                ot = of_ref[rows[j], hs] + o
                ms = jnp.mean(ot * ot, axis=-1, keepdims=True)
                y = ot * lax.rsqrt(ms + EPS) * gain_ref[:, hs]
                o_ref[rows[j], hs] = (y * _silu(g_ref[rows[j], hs].astype(F32))).astype(o_ref.dtype)
            else:
                o_ref[rows[j], hs] = o

    n_trips = nchunk // group
    rows = trip_rows(0)
    qk, rhs = gates(rows)
    dall = decay_sums(rhs)
    for k in range(n_trips):
        levels, o_inter = score_free(rows, qk, dall)
        if k + 1 < n_trips:
            next_rows = trip_rows(k + 1)
            qk, rhs = gates(next_rows)
            dall = decay_sums(rhs)
        finish(rows, levels, o_inter)
        if k + 1 < n_trips:
            rows = next_rows


def _hgrn(q, z, v, gate, of, hg_lower, hg_gain, layer, n_a, reverse, tb, cb, group, unroll):
    t = q.shape[0]
    n_t = t // tb
    depth = hg_lower.shape[1]
    d = 1 if reverse else 0
    msum = jnp.asarray(_decay_sum_matrix(reverse), dtype=BF16)

    def tmap(ti):
        return (n_t - 1 - ti) if reverse else ti

    blk = lambda: pl.BlockSpec((tb, cb), lambda c, ti: (tmap(ti), c))
    in_specs = [blk(), blk(), blk()]
    args = [q, z, v]
    if reverse:
        in_specs += [blk(), blk()]
        args += [gate, of]
    in_specs.append(pl.BlockSpec((None, depth, cb), lambda c, ti: (d, 0, c)))
    args.append(hg_lower)
    if reverse:
        in_specs.append(pl.BlockSpec((None, 1, cb), lambda c, ti: (layer, 0, c)))
        args.append(hg_gain)
    in_specs.append(pl.BlockSpec(msum.shape, lambda c, ti: (0, 0)))
    args.append(msum)
    return pl.pallas_call(
        functools.partial(_hgrn_body, reverse=reverse, layer=layer, n_a=n_a, n_t=n_t, tb=tb, cb=cb,
                          group=group, unroll=unroll),
        grid=(D_MODEL // cb, n_t),
        in_specs=in_specs,
        out_specs=blk(),
        out_shape=jax.ShapeDtypeStruct((t, D_MODEL), BF16 if reverse else F32),
        scratch_shapes=[pltpu.VMEM((cb // HEAD, HEAD, HEAD), F32)],
        compiler_params=pltpu.CompilerParams(
            dimension_semantics=("arbitrary", "arbitrary"), vmem_limit_bytes=VMEM_LIMIT),
        name="hgrn_bwd" if reverse else "hgrn_fwd",
    )(*args)


def _merge_body(*refs, final, off, n_first):
    x_refs = refs[:-9]
    ar_ref, ah_ref, mr_ref, mh_ref, wr_ref, wh_ref, wo_ref, fg_ref, o_ref = refs[-9:]
    if len(x_refs) == 1:
        x = x_refs[0][...]
    else:
        x = jnp.where(pl.program_id(0) + off < n_first, x_refs[0][...], x_refs[1][...])
    yr = jnp.dot(ar_ref[...], wr_ref[...], preferred_element_type=F32)
    yh = jnp.dot(ah_ref[...], wh_ref[...], preferred_element_type=F32)
    merged = _sigmoid(mr_ref[...].astype(F32)) * yr + _sigmoid(mh_ref[...].astype(F32)) * yh
    x = x + jnp.dot(merged.astype(BF16), wo_ref[...], preferred_element_type=F32)
    if final:
        ms = jnp.mean(x * x, axis=-1, keepdims=True)
        x = x * lax.rsqrt(ms + EPS) * fg_ref[...]
    o_ref[...] = x


def _merge(x, a_r, a_h, m_r, m_h, w_dr, w_dh, w_o, final_gain, layer, final, tm, row0, n_rows):
    off = row0 // tm
    x_specs, x_args = _row_specs(x, tm, off)
    n_first = x_args[0].shape[0] // tm
    row = lambda: pl.BlockSpec((tm, D_MODEL), lambda i: (i + off, 0))
    wgt = lambda: pl.BlockSpec((None, D_MODEL, D_MODEL), lambda i: (layer, 0, 0),
                               pipeline_mode=pl.Buffered(1))
    return pl.pallas_call(
        functools.partial(_merge_body, final=final, off=off, n_first=n_first),
        grid=(n_rows // tm,),
        in_specs=x_specs + [row(), row(), row(), row(), wgt(), wgt(), wgt(),
                            pl.BlockSpec((1, D_MODEL), lambda i: (0, 0))],
        out_specs=pl.BlockSpec((tm, D_MODEL), lambda i: (i, 0)),
        out_shape=jax.ShapeDtypeStruct((n_rows, D_MODEL), F32),
        compiler_params=pltpu.CompilerParams(
            dimension_semantics=("arbitrary",), vmem_limit_bytes=VMEM_LIMIT),
        name="merge",
    )(*x_args, a_r, a_h, m_r, m_h, w_dr, w_dh, w_o, final_gain)


def kernel(x_prompt, x_sample, norm_gain, w_in, conv_w, conv_b, rg_wa, rg_ba, rg_wx, rg_bx, rg_lambda,
           hg_lower, hg_norm_gain, w_down_r, w_down_h, w_out, final_gain):
    depth = w_in.shape[0]
    s_a = x_prompt.shape[0] * x_prompt.shape[1]
    s_b = x_sample.shape[0] * x_sample.shape[1]
    assert x_prompt.shape[0] == 1 and x_sample.shape[0] == 1
    t = s_a + s_b
    x = (x_prompt.reshape(s_a, D_MODEL), x_sample.reshape(s_b, D_MODEL))

    tb = _largest_tile(min(s_a, s_b), 1024)
    n_a = s_a // tb
    tm_in = _largest_tile(min(s_a, s_b), 512)
    tm_merge = _largest_tile(min(s_a, s_b), 256)
    rg_cb = 512
    rg_rc = 128
    rg_unroll = min(4, tb // rg_rc)
    hg_group = 4
    hg_unroll = min(4, tb // (CHUNK * hg_group))

    w_rg_b = w_in[:, :, :2 * D_MODEL].astype(BF16)
    w_dr_b = w_down_r.astype(BF16)
    w_dh_b = w_down_h.astype(BF16)
    w_o_b = w_out.astype(BF16)
    w_cat = jnp.concatenate([rg_wa, rg_wx], axis=-1).astype(BF16)
    gain3 = norm_gain.reshape(depth, 1, D_MODEL)
    conv_b3 = conv_b.reshape(depth, 1, D_MODEL)
    ba4 = rg_ba.reshape(depth, 2, 1, D_MODEL)
    bx4 = rg_bx.reshape(depth, 2, 1, D_MODEL)
    lam4 = rg_lambda.reshape(depth, 2, 1, D_MODEL)
    hgain3 = hg_norm_gain.reshape(depth, 1, D_MODEL)
    fgain = final_gain.reshape(1, D_MODEL)

    for l in range(depth):
        proj, hn = _inproj(x, t, gain3, w_rg_b, l, tm_in if l == 0 else _largest_tile(t, 1024), 1024)
        rg = (conv_w, conv_b3, w_cat, ba4, bx4, lam4, l, n_a)
        hf, q, zf, v = _rglru(proj, None, hn, w_in, (COL_Q, COL_ZF, COL_V), *rg, False,
                              tb, rg_cb // 2, rg_rc, min(2 * rg_unroll, tb // rg_rc))
        a_r, zb, gh, m_r, m_h = _rglru(proj, hf, hn, w_in, (COL_ZB, COL_GH, COL_MR, COL_MH), *rg, True,
                                       tb, rg_cb, rg_rc, rg_unroll)
        of = _hgrn(q, zf, v, None, None, hg_lower, hgain3, l, n_a, False, tb, 512, hg_group, hg_unroll)
        a_h = _hgrn(q, zb, v, gh, of, hg_lower, hgain3, l, n_a, True, tb, 512, hg_group, hg_unroll)
        mg = (a_r, a_h, m_r, m_h, w_dr_b, w_dh_b, w_o_b, fgain, l)
        if l < depth - 1:
            x = _merge(x, *mg, False, tm_merge, 0, t)
        else:
            y_a = _merge(x, *mg, True, tm_merge, 0, s_a)
            y_b = _merge(x, *mg, True, tm_merge, s_a, s_b)
    return (y_a.reshape(x_prompt.shape), y_b.reshape(x_sample.shape))
```

```python
import functools

import numpy as np
import jax
import jax.numpy as jnp
from jax import lax
from jax.experimental import pallas as pl
from jax.experimental.pallas import tpu as pltpu

F32 = jnp.float32
BF16 = jnp.bfloat16

D_MODEL = 2048
HEAD = 128
N_SPLITS = 9
COL_XR, COL_GR, COL_Q, COL_ZF, COL_ZB, COL_V, COL_GH, COL_MR, COL_MH = range(N_SPLITS)
EPS = 1e-6
RG_C = 8.0
CHUNK = 64
LEVELS = (1, 2, 4, 8, 16, 32)
LOG2E = 1.4426950408889634
HALO = 16
VMEM_LIMIT = 56 * 1024 * 1024

_NT = (((1,), (1,)), ((), ()))
_TN = (((0,), (0,)), ((), ()))


def _sigmoid(x):
    return 0.5 * jnp.tanh(0.5 * x) + 0.5


def _silu(x):
    hx = 0.5 * x
    return hx * (1.0 + jnp.tanh(hx))


def _softplus(x):
    return jnp.maximum(x, 0.0) + jnp.log1p(jnp.exp(-jnp.abs(x)))


def _split3(x):
    hi = x.astype(BF16)
    r1 = x - hi.astype(F32)
    mid = r1.astype(BF16)
    lo = (r1 - mid.astype(F32)).astype(BF16)
    return hi, mid, lo


def _largest_tile(n, cap):
    t = cap
    while n % t:
        t //= 2
    return t


def _row_specs(x, tm, off=0):
    if not isinstance(x, tuple):
        return [pl.BlockSpec((tm, D_MODEL), lambda i, *_: (i + off, 0))], [x]
    n_first = x[0].shape[0] // tm
    return [pl.BlockSpec((tm, D_MODEL), lambda i, *_: (jnp.minimum(i + off, n_first - 1), 0)),
            pl.BlockSpec((tm, D_MODEL), lambda i, *_: (jnp.maximum(i + off - n_first, 0), 0))], list(x)


def _inproj_body(*refs, tm, n_first):
    x_refs = refs[:-4]
    g_ref, w_ref, o_ref, h_ref = refs[-4:]
    i = pl.program_id(0)

    def normalise(x_ref):
        rows = 128
        def norm(r, c):
            sl = pl.ds(pl.multiple_of(r * rows, rows), rows)
            x = x_ref[sl, :]
            ms = jnp.mean(x * x, axis=-1, keepdims=True)
            h_ref[sl, :] = (x * lax.rsqrt(ms + EPS) * g_ref[...]).astype(BF16)
            return c
        lax.fori_loop(0, tm // rows, norm, 0)

    first_col = pl.program_id(1) == 0
    if len(x_refs) == 1:
        pl.when(first_col)(lambda: normalise(x_refs[0]))
    else:
        pl.when(jnp.logical_and(first_col, i < n_first))(lambda: normalise(x_refs[0]))
        pl.when(jnp.logical_and(first_col, i >= n_first))(lambda: normalise(x_refs[1]))

    o_ref[...] = jnp.dot(h_ref[...], w_ref[...], preferred_element_type=F32).astype(o_ref.dtype)


def _inproj(x, t, gain, w_in, layer, tm, tn):
    x_specs, x_args = _row_specs(x, tm)
    n_first = x_args[0].shape[0] // tm
    return pl.pallas_call(
        functools.partial(_inproj_body, tm=tm, n_first=n_first),
        grid=(t // tm, 2 * D_MODEL // tn),
        in_specs=x_specs + [
            pl.BlockSpec((None, 1, D_MODEL), lambda i, j: (layer, 0, 0)),
            pl.BlockSpec((None, D_MODEL, tn), lambda i, j: (layer, 0, j)),
        ],
        out_specs=[pl.BlockSpec((tm, tn), lambda i, j: (i, j)),
                   pl.BlockSpec((tm, D_MODEL), lambda i, j: (i, 0))],
        out_shape=[jax.ShapeDtypeStruct((t, 2 * D_MODEL), BF16),
                   jax.ShapeDtypeStruct((t, D_MODEL), BF16)],
        compiler_params=pltpu.CompilerParams(
            dimension_semantics=("arbitrary", "arbitrary"), vmem_limit_bytes=VMEM_LIMIT),
        name="inproj",
    )(*x_args, gain, w_in)


def _rglru_body(*refs, reverse, n_extra, n_a, n_t, tb, cb, rc, unroll):
    refs = list(refs)
    x_ref, xp_ref, xn_ref = refs[:3]
    del refs[:3]
    if reverse:
        gr_ref, hf_ref = refs[:2]
        del refs[:2]
    cw_ref, cbias_ref, w_ref, ba_ref, bx_ref, lam_ref, hn_ref = refs[:7]
    wx_refs = refs[7:7 + n_extra]
    o_ref = refs[7 + n_extra]
    px_refs = refs[8 + n_extra:8 + 2 * n_extra]
    xs_ref, a_s, h_s, carry_ref, wb_ref = refs[8 + 2 * n_extra:]

    nh = cb // HEAD
    nchunk = tb // rc
    nv = rc // 8
    t = pl.program_id(1)
    tix = (n_t - 1 - t) if reverse else t
    seq_first = jnp.logical_or(tix == 0, tix == n_a)
    seq_last = jnp.logical_or(tix == n_a - 1, tix == n_t - 1)
    start = seq_last if reverse else seq_first

    @pl.when(start)
    def _():
        carry_ref[...] = jnp.zeros_like(carry_ref)

    @pl.when(t == 0)
    def _():
        for e, wx_ref in enumerate(wx_refs):
            wb_ref[e] = wx_ref[...].astype(BF16)

    pm = jnp.where(seq_first, 0.0, 1.0).astype(F32)
    nm = jnp.where(seq_last, 0.0, 1.0).astype(F32)
    xs_ref[0:8, :] = xp_ref[...].astype(F32)[HALO - 8:HALO, :] * pm
    xs_ref[8:8 + tb, :] = x_ref[...].astype(F32)
    xs_ref[8 + tb:16 + tb, :] = xn_ref[...].astype(F32)[0:8, :] * nm

    neg_c_sp = (-RG_C * LOG2E) * _softplus(-lam_ref[...])
    sub = lax.broadcasted_iota(jnp.int32, (nv, 8, HEAD), 1)

    def chunk(k, c):
        kk = (nchunk - 1 - k) if reverse else k
        r0 = pl.multiple_of(kk * rc, rc)
        rows = pl.ds(r0, rc)
        hn = hn_ref[rows, :]
        extra = list(enumerate(px_refs))

        def project(pairs):
            for e, px_ref in pairs:
                px_ref[rows, :] = jnp.dot(hn, wb_ref[e], preferred_element_type=F32).astype(px_ref.dtype)

        for h in range(nh):
            project(extra[h::nh + 1])
            hs = slice(h * HEAD, (h + 1) * HEAD)
            xw = xs_ref[pl.ds(r0, rc + 16), hs].reshape(nv + 2, 8, HEAD)

            def tap(s):
                if s == 0:
                    return xw[1:nv + 1]
                rt = pltpu.roll(xw, s % 8, axis=1)
                if s > 0:
                    return jnp.where(sub < s, rt[0:nv], rt[1:nv + 1])
                return jnp.where(sub < 8 + s, rt[1:nv + 1], rt[2:nv + 2])

            xc = (cbias_ref[:, hs] + tap(2) * cw_ref[0:1, hs] + tap(1) * cw_ref[1:2, hs]
                  + tap(0) * cw_ref[2:3, hs] + tap(-1) * cw_ref[3:4, hs]).reshape(rc, HEAD)
            g = jnp.dot(xc.astype(BF16), w_ref[h], preferred_element_type=F32)
            r = _sigmoid(g[:, :HEAD] + ba_ref[:, hs])
            i = _sigmoid(g[:, HEAD:] + bx_ref[:, hs])
            a = jnp.exp2(r * neg_c_sp[:, hs])
            bv = jnp.sqrt(1.0 - a * a) * (i * xc)
            a = a.reshape(nv, 8, HEAD)
            bv = bv.reshape(nv, 8, HEAD)
            for d in (1, 2, 4):
                a_sh = pltpu.roll(a, (8 - d) if reverse else d, axis=1)
                b_sh = pltpu.roll(bv, (8 - d) if reverse else d, axis=1)
                valid = (sub < 8 - d) if reverse else (sub >= d)
                bv = jnp.where(valid, a * b_sh + bv, bv)
                a = jnp.where(valid, a * a_sh, a)
            a_s[h] = a.reshape(rc, HEAD)
            h_s[h] = bv.reshape(rc, HEAD)

        project(extra[nh::nh + 1])

        def step(i, cs):
            ii = (nv - 1 - i) if reverse else i
            sl = pl.ds(pl.multiple_of(ii * 8, 8), 8)
            out = []
            for h in range(nh):
                hv = h_s[h, sl, :] + a_s[h, sl, :] * cs[h]
                h_s[h, sl, :] = hv
                out.append(hv[0:1, :] if reverse else hv[7:8, :])
            return tuple(out)

        cs = tuple(carry_ref[:, h * HEAD:(h + 1) * HEAD] for h in range(nh))
        cs = lax.fori_loop(0, nv, step, cs, unroll=True)
        for h in range(nh):
            hs = slice(h * HEAD, (h + 1) * HEAD)
            carry_ref[:, hs] = cs[h]
            if reverse:
                gate = gr_ref[rows, hs].astype(F32)
                o_ref[rows, hs] = ((hf_ref[rows, hs] + h_s[h]) * _silu(gate)).astype(o_ref.dtype)
            else:
                o_ref[rows, hs] = h_s[h]

        return c

    lax.fori_loop(0, nchunk, chunk, 0, unroll=unroll)


def _rglru(proj, hf, hn, w_in, extra_cols, conv_w, conv_b, w_cat, ba, bx, lam, layer, n_a, reverse,
           tb, cb, rc, unroll):
    t = proj.shape[0]
    n_t = t // tb
    ncg = D_MODEL // cb
    hb = tb // HALO
    nhalo = t // HALO
    d = 1 if reverse else 0
    n_extra = len(extra_cols)

    def tmap(ti):
        return (n_t - 1 - ti) if reverse else ti

    blk = lambda col: pl.BlockSpec((tb, cb), lambda c, ti: (tmap(ti), col * ncg + c))
    vec = lambda: pl.BlockSpec((None, 1, cb), lambda c, ti: (layer, 0, c))
    dvec = lambda: pl.BlockSpec((None, None, 1, cb), lambda c, ti: (layer, d, 0, c))
    in_specs = [
        blk(COL_XR),
        pl.BlockSpec((HALO, cb), lambda c, ti: (jnp.maximum(tmap(ti) * hb - 1, 0), c)),
        pl.BlockSpec((HALO, cb), lambda c, ti: (jnp.minimum((tmap(ti) + 1) * hb, nhalo - 1), c)),
    ]
    args = [proj, proj, proj]
    if reverse:
        in_specs += [blk(COL_GR), pl.BlockSpec((tb, cb), lambda c, ti: (tmap(ti), c))]
        args += [proj, hf]
    in_specs += [
        pl.BlockSpec((None, 4, cb), lambda c, ti: (layer, 0, c)),
        vec(),
        pl.BlockSpec((None, None, cb // HEAD, HEAD, 2 * HEAD), lambda c, ti: (layer, d, c, 0, 0)),
        dvec(), dvec(), dvec(),
        pl.BlockSpec((tb, D_MODEL), lambda c, ti: (tmap(ti), 0)),
    ]
    args += [conv_w, conv_b, w_cat, ba, bx, lam, hn]
    for col in extra_cols:
        in_specs.append(pl.BlockSpec((None, D_MODEL, cb), lambda c, ti, col=col: (layer, 0, col * ncg + c),
                                     pipeline_mode=pl.Buffered(1)))
        args.append(w_in)
    out_blk = lambda: pl.BlockSpec((tb, cb), lambda c, ti: (tmap(ti), c))
    nh = cb // HEAD
    return pl.pallas_call(
        functools.partial(_rglru_body, reverse=reverse, n_extra=n_extra, n_a=n_a, n_t=n_t,
                          tb=tb, cb=cb, rc=rc, unroll=unroll),
        grid=(ncg, n_t),
        in_specs=in_specs,
        out_specs=[out_blk() for _ in range(1 + n_extra)],
        out_shape=[jax.ShapeDtypeStruct((t, D_MODEL), BF16 if reverse else F32)]
        + [jax.ShapeDtypeStruct((t, D_MODEL), BF16) for _ in range(n_extra)],
        scratch_shapes=[
            pltpu.VMEM((tb + 16, cb), F32),
            pltpu.VMEM((nh, rc, HEAD), F32),
            pltpu.VMEM((nh, rc, HEAD), F32),
            pltpu.VMEM((1, cb), F32),
            pltpu.VMEM((n_extra, D_MODEL, cb), BF16),
        ],
        compiler_params=pltpu.CompilerParams(
            dimension_semantics=("arbitrary", "arbitrary"), vmem_limit_bytes=VMEM_LIMIT),
        name="rglru_bwd" if reverse else "rglru_fwd",
    )(*args)


def _decay_sum_matrix(reverse):
    t = np.arange(CHUNK)[:, None]
    u = np.arange(CHUNK)[None, :]
    blocks = [u >= t] if reverse else [u <= t]
    for m in LEVELS[1:]:
        upper = (t & m) != 0
        if reverse:
            rho = (t // (2 * m)) * (2 * m) + m
            blocks.append(np.where(upper, (u >= rho) & (u < t), (u >= t) & (u < rho)))
        else:
            rho = (t // (2 * m)) * (2 * m) + m - 1
            blocks.append(np.where(upper, (u > rho) & (u <= t), (u > t) & (u <= rho)))
    mat = np.concatenate(blocks, axis=0).astype(np.float32)
    return np.concatenate([mat, mat, mat], axis=1)


def _hgrn_body(*refs, reverse, layer, n_a, n_t, tb, cb, group, unroll):
    if reverse:
        (q_ref, z_ref, v_ref, g_ref, of_ref, lower_ref, gain_ref, m_ref, o_ref, st_ref) = refs
    else:
        (q_ref, z_ref, v_ref, lower_ref, m_ref, o_ref, st_ref) = refs
    nh = cb // HEAD
    nchunk = tb // CHUNK
    t = pl.program_id(1)
    tix = (n_t - 1 - t) if reverse else t
    if reverse:
        start = jnp.logical_or(tix == n_a - 1, tix == n_t - 1)
    else:
        start = jnp.logical_or(tix == 0, tix == n_a)

    @pl.when(start)
    def _():
        st_ref[...] = jnp.zeros_like(st_ref)

    low = lower_ref[...]
    e = jnp.exp(low - jnp.max(low, axis=0, keepdims=True))
    den = jnp.sum(e, axis=0, keepdims=True)
    if layer == 0:
        lb = jnp.zeros_like(den)
    else:
        lb = jnp.sum(e[1:layer + 1, :], axis=0, keepdims=True) / den
    log_lb = jnp.log(lb)
    log1m_lb = jnp.log1p(-lb)
    half_1m_lb = 0.5 * (1.0 - lb)

    ti = lax.broadcasted_iota(jnp.int32, (CHUNK, CHUNK), 0)
    si = lax.broadcasted_iota(jnp.int32, (CHUNK, CHUNK), 1)
    txs = ti ^ si
    causal = (ti <= si) if reverse else (ti >= si)
    rowi = lax.broadcasted_iota(jnp.int32, (CHUNK, HEAD), 0)
    q_side = [((rowi & m) == 0) if reverse else ((rowi & m) != 0) for m in LEVELS]

    hsl = [slice(h * HEAD, (h + 1) * HEAD) for h in range(nh)]
    items = [(j, h) for j in range(group) for h in range(nh)]

    def trip_rows(k):
        rows = []
        for j in range(group):
            kc = k * group + j
            kk = (nchunk - 1 - kc) if reverse else kc
            rows.append(pl.ds(kk * CHUNK, CHUNK))
        return rows

    def gates(rows):
        qk = {}
        rhs = {}
        for j in range(group):
            for pair in range(nh // 2):
                parts = []
                for h in (2 * pair, 2 * pair + 1):
                    hs = hsl[h]
                    z = z_ref[rows[j], hs].astype(F32)
                    q = _silu(q_ref[rows[j], hs].astype(F32))
                    ez = jnp.exp(-jnp.abs(z))
                    log_sig = jnp.minimum(z, 0.0) - jnp.log(1.0 + ez)
                    if layer == 0:
                        logf = log_sig
                    else:
                        u = log1m_lb[:, hs] + log_sig
                        la = log_lb[:, hs]
                        logf = jnp.maximum(la, u) + jnp.log(1.0 + jnp.exp(-jnp.abs(la - u)))
                    key = half_1m_lb[:, hs] - half_1m_lb[:, hs] * jnp.tanh(0.5 * z)
                    logf2 = logf * LOG2E
                    qk[(j, h)] = (q, key, logf2)
                    parts.append(_split3(logf2))
                rhs[(j, pair)] = jnp.concatenate(
                    [jnp.concatenate([parts[0][i], parts[1][i]], axis=1) for i in range(3)], axis=0)
        return qk, rhs

    def decay_sums(rhs):
        dall = {}
        for (j, pair), r in rhs.items():
            dpair = jnp.dot(m_ref[...], r, preferred_element_type=F32)
            dall[(j, 2 * pair)] = dpair[:, :HEAD]
            dall[(j, 2 * pair + 1)] = dpair[:, HEAD:]
        return dall

    def score_free(rows, qk, dall):
        levels = {}
        o_inter = {}
        for j, h in items:
            q, key, logf2 = qk[(j, h)]
            d = dall[(j, h)]
            q16 = q.astype(BF16)
            k16 = key.astype(BF16)
            p = [lax.dot_general(q16, k16, _NT, preferred_element_type=F32)]
            for li in range(len(LEVELS)):
                if li == 0:
                    d_l = jnp.where(q_side[0], logf2, 0.0)
                else:
                    d_l = d[li * CHUNK:(li + 1) * CHUNK, :]
                w_l = jnp.where(q_side[li], q16, k16) * jnp.exp2(d_l.astype(BF16))
                p.append(lax.dot_general(w_l, w_l, _NT, preferred_element_type=F32))
            levels[(j, h)] = p

            b2 = d[0:CHUNK, :]
            st = st_ref[h]
            o_inter[(j, h)] = ((q * jnp.exp2(b2)).astype(BF16), st.astype(BF16))
            total = b2[0:1, :] if reverse else b2[CHUNK - 1:CHUNK, :]
            kd = (key * jnp.exp2(total - b2)).astype(BF16)
            dec = jnp.transpose(jnp.broadcast_to(jnp.exp2(total), (HEAD, HEAD)))
            st_ref[h] = st * dec + lax.dot_general(
                kd, v_ref[rows[j], hsl[h]], _TN, preferred_element_type=F32)
        return levels, o_inter

    def finish(rows, levels, o_inter):
        for j, h in items:
            hs = hsl[h]
            scores = levels[(j, h)][0].astype(BF16)
            for li, m in enumerate(LEVELS):
                scores = jnp.where(txs >= m, levels[(j, h)][li + 1].astype(BF16), scores)
            scores = jnp.where(causal, scores, jnp.zeros_like(scores))
            q_in, st16 = o_inter[(j, h)]
            o = jnp.dot(jnp.concatenate([q_in, scores], axis=1),
                        jnp.concatenate([st16, v_ref[rows[j], hs]], axis=0), preferred_element_type=F32)
            if reverse:
                ot = of_ref[rows[j], hs] + o
                ms = jnp.mean(ot * ot, axis=-1, keepdims=True)
                y = ot * lax.rsqrt(ms + EPS) * gain_ref[:, hs]
                o_ref[rows[j], hs] = (y * _silu(g_ref[rows[j], hs].astype(F32))).astype(o_ref.dtype)
            else:
                o_ref[rows[j], hs] = o

    n_trips = nchunk // group
    rows = trip_rows(0)
    qk, rhs = gates(rows)
    dall = decay_sums(rhs)
    for k in range(n_trips):
        levels, o_inter = score_free(rows, qk, dall)
        if k + 1 < n_trips:
            next_rows = trip_rows(k + 1)
            qk, rhs = gates(next_rows)
            dall = decay_sums(rhs)
        finish(rows, levels, o_inter)
        if k + 1 < n_trips:
            rows = next_rows


def _hgrn(q, z, v, gate, of, hg_lower, hg_gain, layer, n_a, reverse, tb, cb, group, unroll):
    t = q.shape[0]
    n_t = t // tb
    depth = hg_lower.shape[1]
    d = 1 if reverse else 0
    msum = jnp.asarray(_decay_sum_matrix(reverse), dtype=BF16)

    def tmap(ti):
        return (n_t - 1 - ti) if reverse else ti

    blk = lambda: pl.BlockSpec((tb, cb), lambda c, ti: (tmap(ti), c))
    in_specs = [blk(), blk(), blk()]
    args = [q, z, v]
    if reverse:
        in_specs += [blk(), blk()]
        args += [gate, of]
    in_specs.append(pl.BlockSpec((None, depth, cb), lambda c, ti: (d, 0, c)))
    args.append(hg_lower)
    if reverse:
        in_specs.append(pl.BlockSpec((None, 1, cb), lambda c, ti: (layer, 0, c)))
        args.append(hg_gain)
    in_specs.append(pl.BlockSpec(msum.shape, lambda c, ti: (0, 0)))
    args.append(msum)
    return pl.pallas_call(
        functools.partial(_hgrn_body, reverse=reverse, layer=layer, n_a=n_a, n_t=n_t, tb=tb, cb=cb,
                          group=group, unroll=unroll),
        grid=(D_MODEL // cb, n_t),
        in_specs=in_specs,
        out_specs=blk(),
        out_shape=jax.ShapeDtypeStruct((t, D_MODEL), BF16 if reverse else F32),
        scratch_shapes=[pltpu.VMEM((cb // HEAD, HEAD, HEAD), F32)],
        compiler_params=pltpu.CompilerParams(
            dimension_semantics=("arbitrary", "arbitrary"), vmem_limit_bytes=VMEM_LIMIT),
        name="hgrn_bwd" if reverse else "hgrn_fwd",
    )(*args)


def _merge_body(*refs, final, off, n_first):
    x_refs = refs[:-9]
    ar_ref, ah_ref, mr_ref, mh_ref, wr_ref, wh_ref, wo_ref, fg_ref, o_ref = refs[-9:]
    if len(x_refs) == 1:
        x = x_refs[0][...]
    else:
        x = jnp.where(pl.program_id(0) + off < n_first, x_refs[0][...], x_refs[1][...])
    yr = jnp.dot(ar_ref[...], wr_ref[...], preferred_element_type=F32)
    yh = jnp.dot(ah_ref[...], wh_ref[...], preferred_element_type=F32)
    merged = _sigmoid(mr_ref[...].astype(F32)) * yr + _sigmoid(mh_ref[...].astype(F32)) * yh
    x = x + jnp.dot(merged.astype(BF16), wo_ref[...], preferred_element_type=F32)
    if final:
        ms = jnp.mean(x * x, axis=-1, keepdims=True)
        x = x * lax.rsqrt(ms + EPS) * fg_ref[...]
    o_ref[...] = x


def _merge(x, a_r, a_h, m_r, m_h, w_dr, w_dh, w_o, final_gain, layer, final, tm, row0, n_rows):
    off = row0 // tm
    x_specs, x_args = _row_specs(x, tm, off)
    n_first = x_args[0].shape[0] // tm
    row = lambda: pl.BlockSpec((tm, D_MODEL), lambda i: (i + off, 0))
    wgt = lambda: pl.BlockSpec((None, D_MODEL, D_MODEL), lambda i: (layer, 0, 0),
                               pipeline_mode=pl.Buffered(1))
    return pl.pallas_call(
        functools.partial(_merge_body, final=final, off=off, n_first=n_first),
        grid=(n_rows // tm,),
        in_specs=x_specs + [row(), row(), row(), row(), wgt(), wgt(), wgt(),
                            pl.BlockSpec((1, D_MODEL), lambda i: (0, 0))],
        out_specs=pl.BlockSpec((tm, D_MODEL), lambda i: (i, 0)),
        out_shape=jax.ShapeDtypeStruct((n_rows, D_MODEL), F32),
        compiler_params=pltpu.CompilerParams(
            dimension_semantics=("arbitrary",), vmem_limit_bytes=VMEM_LIMIT),
        name="merge",
    )(*x_args, a_r, a_h, m_r, m_h, w_dr, w_dh, w_o, final_gain)


def kernel(x_prompt, x_sample, norm_gain, w_in, conv_w, conv_b, rg_wa, rg_ba, rg_wx, rg_bx, rg_lambda,
           hg_lower, hg_norm_gain, w_down_r, w_down_h, w_out, final_gain):
    depth = w_in.shape[0]
    s_a = x_prompt.shape[0] * x_prompt.shape[1]
    s_b = x_sample.shape[0] * x_sample.shape[1]
    assert x_prompt.shape[0] == 1 and x_sample.shape[0] == 1
    t = s_a + s_b
    x = (x_prompt.reshape(s_a, D_MODEL), x_sample.reshape(s_b, D_MODEL))

    tb = _largest_tile(min(s_a, s_b), 1024)
    n_a = s_a // tb
    tm_in = _largest_tile(min(s_a, s_b), 512)
    tm_merge = _largest_tile(min(s_a, s_b), 256)
    rg_cb = 512
    rg_rc = 128
    rg_unroll = min(8, tb // rg_rc)
    hg_group = 4
    hg_unroll = min(4, tb // (CHUNK * hg_group))

    w_rg_b = w_in[:, :, :2 * D_MODEL].astype(BF16)
    w_dr_b = w_down_r.astype(BF16)
    w_dh_b = w_down_h.astype(BF16)
    w_o_b = w_out.astype(BF16)
    w_cat = jnp.concatenate([rg_wa, rg_wx], axis=-1).astype(BF16)
    gain3 = norm_gain.reshape(depth, 1, D_MODEL)
    conv_b3 = conv_b.reshape(depth, 1, D_MODEL)
    ba4 = rg_ba.reshape(depth, 2, 1, D_MODEL)
    bx4 = rg_bx.reshape(depth, 2, 1, D_MODEL)
    lam4 = rg_lambda.reshape(depth, 2, 1, D_MODEL)
    hgain3 = hg_norm_gain.reshape(depth, 1, D_MODEL)
    fgain = final_gain.reshape(1, D_MODEL)

    for l in range(depth):
        proj, hn = _inproj(x, t, gain3, w_rg_b, l, tm_in if l == 0 else _largest_tile(t, 1024), 1024)
        rg = (conv_w, conv_b3, w_cat, ba4, bx4, lam4, l, n_a)
        hf, q, zf, v = _rglru(proj, None, hn, w_in, (COL_Q, COL_ZF, COL_V), *rg, False,
                              tb, rg_cb // 2, rg_rc, min(2 * rg_unroll, tb // rg_rc))
        a_r, zb, gh, m_r, m_h = _rglru(proj, hf, hn, w_in, (COL_ZB, COL_GH, COL_MR, COL_MH), *rg, True,
                                       tb, rg_cb, rg_rc, rg_unroll)
        of = _hgrn(q, zf, v, None, None, hg_lower, hgain3, l, n_a, False, tb, 512, hg_group, hg_unroll)
        a_h = _hgrn(q, zb, v, gh, of, hg_lower, hgain3, l, n_a, True, tb, 512, hg_group, hg_unroll)
        mg = (a_r, a_h, m_r, m_h, w_dr_b, w_dh_b, w_o_b, fgain, l)
        if l < depth - 1:
            x = _merge(x, *mg, False, tm_merge, 0, t)
        else:
            y_a = _merge(x, *mg, True, tm_merge, 0, s_a)
            y_b = _merge(x, *mg, True, tm_merge, s_a, s_b)
    return (y_a.reshape(x_prompt.shape), y_b.reshape(x_sample.shape))
```

```python
import functools

import numpy as np
import jax
import jax.numpy as jnp
from jax import lax
from jax.experimental import pallas as pl
from jax.experimental.pallas import tpu as pltpu

F32 = jnp.float32
BF16 = jnp.bfloat16

D_MODEL = 2048
HEAD = 128
N_SPLITS = 9
COL_XR, COL_GR, COL_Q, COL_ZF, COL_ZB, COL_V, COL_GH, COL_MR, COL_MH = range(N_SPLITS)
EPS = 1e-6
RG_C = 8.0
CHUNK = 64
LEVELS = (1, 2, 4, 8, 16, 32)
LOG2E = 1.4426950408889634
HALO = 16
VMEM_LIMIT = 56 * 1024 * 1024

TIME_BLOCK = 1024
RG_ROWS = 128
RG_FWD_CHANNELS = 256
RG_BWD_CHANNELS = 512
HG_CHANNELS = 512
HG_GROUP = 2
INPROJ_ROWS = 1024
INPROJ_ROWS_SPLIT_INPUT = 512
INPROJ_COLS = 1024
MERGE_ROWS = 256

_NT = (((1,), (1,)), ((), ()))
_TN = (((0,), (0,)), ((), ()))


def _sigmoid(x):
    return 0.5 * jnp.tanh(0.5 * x) + 0.5


def _silu(x):
    hx = 0.5 * x
    return hx * (1.0 + jnp.tanh(hx))


def _softplus(x):
    return jnp.maximum(x, 0.0) + jnp.log1p(jnp.exp(-jnp.abs(x)))


def _split3(x):
    hi = x.astype(BF16)
    r1 = x - hi.astype(F32)
    mid = r1.astype(BF16)
    lo = (r1 - mid.astype(F32)).astype(BF16)
    return hi, mid, lo


def _largest_tile(n, cap):
    t = cap
    while n % t:
        t //= 2
    return t


def _row_specs(x, tm, off=0):
    if not isinstance(x, tuple):
        return [pl.BlockSpec((tm, D_MODEL), lambda i, *_: (i + off, 0))], [x]
    n_first = x[0].shape[0] // tm
    return [pl.BlockSpec((tm, D_MODEL), lambda i, *_: (jnp.minimum(i + off, n_first - 1), 0)),
            pl.BlockSpec((tm, D_MODEL), lambda i, *_: (jnp.maximum(i + off - n_first, 0), 0))], list(x)


def _inproj_body(*refs, tm, n_first):
    x_refs = refs[:-4]
    g_ref, w_ref, o_ref, h_ref = refs[-4:]
    i = pl.program_id(0)

    def normalise(x_ref):
        rows = 128
        def norm(r, c):
            sl = pl.ds(pl.multiple_of(r * rows, rows), rows)
            x = x_ref[sl, :]
            ms = jnp.mean(x * x, axis=-1, keepdims=True)
            h_ref[sl, :] = (x * lax.rsqrt(ms + EPS) * g_ref[...]).astype(BF16)
            return c
        lax.fori_loop(0, tm // rows, norm, 0)

    first_col = pl.program_id(1) == 0
    if len(x_refs) == 1:
        pl.when(first_col)(lambda: normalise(x_refs[0]))
    else:
        pl.when(jnp.logical_and(first_col, i < n_first))(lambda: normalise(x_refs[0]))
        pl.when(jnp.logical_and(first_col, i >= n_first))(lambda: normalise(x_refs[1]))

    o_ref[...] = jnp.dot(h_ref[...], w_ref[...], preferred_element_type=F32).astype(o_ref.dtype)


def _inproj(x, t, gain, w_in, layer, tm, tn):
    x_specs, x_args = _row_specs(x, tm)
    n_first = x_args[0].shape[0] // tm
    return pl.pallas_call(
        functools.partial(_inproj_body, tm=tm, n_first=n_first),
        grid=(t // tm, 2 * D_MODEL // tn),
        in_specs=x_specs + [
            pl.BlockSpec((None, 1, D_MODEL), lambda i, j: (layer, 0, 0)),
            pl.BlockSpec((None, D_MODEL, tn), lambda i, j: (layer, 0, j)),
        ],
        out_specs=[pl.BlockSpec((tm, tn), lambda i, j: (i, j)),
                   pl.BlockSpec((tm, D_MODEL), lambda i, j: (i, 0))],
        out_shape=[jax.ShapeDtypeStruct((t, 2 * D_MODEL), BF16),
                   jax.ShapeDtypeStruct((t, D_MODEL), BF16)],
        compiler_params=pltpu.CompilerParams(
            dimension_semantics=("arbitrary", "arbitrary"), vmem_limit_bytes=VMEM_LIMIT),
        name="inproj",
    )(*x_args, gain, w_in)


def _rglru_body(*refs, reverse, n_extra, n_a, n_t, tb, cb, rc, unroll):
    refs = list(refs)
    x_ref, xp_ref, xn_ref = refs[:3]
    del refs[:3]
    if reverse:
        gr_ref, hf_ref = refs[:2]
        del refs[:2]
    cw_ref, cbias_ref, w_ref, ba_ref, bx_ref, lam_ref, hn_ref = refs[:7]
    wx_refs = refs[7:7 + n_extra]
    o_ref = refs[7 + n_extra]
    px_refs = refs[8 + n_extra:8 + 2 * n_extra]
    xs_ref, a_s, h_s, carry_ref, wb_ref = refs[8 + 2 * n_extra:]

    nh = cb // HEAD
    nchunk = tb // rc
    nv = rc // 8
    t = pl.program_id(1)
    tix = (n_t - 1 - t) if reverse else t
    seq_first = jnp.logical_or(tix == 0, tix == n_a)
    seq_last = jnp.logical_or(tix == n_a - 1, tix == n_t - 1)
    start = seq_last if reverse else seq_first

    @pl.when(start)
    def _():
        carry_ref[...] = jnp.zeros_like(carry_ref)

    @pl.when(t == 0)
    def _():
        for e, wx_ref in enumerate(wx_refs):
            wb_ref[e] = wx_ref[...].astype(BF16)

    pm = jnp.where(seq_first, 0.0, 1.0).astype(F32)
    nm = jnp.where(seq_last, 0.0, 1.0).astype(F32)
    xs_ref[0:8, :] = xp_ref[...].astype(F32)[HALO - 8:HALO, :] * pm
    xs_ref[8:8 + tb, :] = x_ref[...].astype(F32)
    xs_ref[8 + tb:16 + tb, :] = xn_ref[...].astype(F32)[0:8, :] * nm

    neg_c_sp = (-RG_C * LOG2E) * _softplus(-lam_ref[...])
    sub = lax.broadcasted_iota(jnp.int32, (nv, 8, HEAD), 1)

    def chunk(k, c):
        kk = (nchunk - 1 - k) if reverse else k
        r0 = pl.multiple_of(kk * rc, rc)
        rows = pl.ds(r0, rc)
        hn = hn_ref[rows, :]
        extra = list(enumerate(px_refs))

        def project(pairs):
            for e, px_ref in pairs:
                px_ref[rows, :] = jnp.dot(hn, wb_ref[e], preferred_element_type=F32).astype(px_ref.dtype)

        for h in range(nh):
            project(extra[h::nh + 1])
            hs = slice(h * HEAD, (h + 1) * HEAD)
            xw = xs_ref[pl.ds(r0, rc + 16), hs].reshape(nv + 2, 8, HEAD)

            def tap(s):
                if s == 0:
                    return xw[1:nv + 1]
                rt = pltpu.roll(xw, s % 8, axis=1)
                if s > 0:
                    return jnp.where(sub < s, rt[0:nv], rt[1:nv + 1])
                return jnp.where(sub < 8 + s, rt[1:nv + 1], rt[2:nv + 2])

            xc = (cbias_ref[:, hs] + tap(2) * cw_ref[0:1, hs] + tap(1) * cw_ref[1:2, hs]
                  + tap(0) * cw_ref[2:3, hs] + tap(-1) * cw_ref[3:4, hs]).reshape(rc, HEAD)
            g = jnp.dot(xc.astype(BF16), w_ref[h], preferred_element_type=F32)
            r = _sigmoid(g[:, :HEAD] + ba_ref[:, hs])
            i = _sigmoid(g[:, HEAD:] + bx_ref[:, hs])
            a = jnp.exp2(r * neg_c_sp[:, hs])
            bv = jnp.sqrt(1.0 - a * a) * (i * xc)
            a = a.reshape(nv, 8, HEAD)
            bv = bv.reshape(nv, 8, HEAD)
            for d in (1, 2, 4):
                a_sh = pltpu.roll(a, (8 - d) if reverse else d, axis=1)
                b_sh = pltpu.roll(bv, (8 - d) if reverse else d, axis=1)
                valid = (sub < 8 - d) if reverse else (sub >= d)
                bv = jnp.where(valid, a * b_sh + bv, bv)
                a = jnp.where(valid, a * a_sh, a)
            a_s[h] = a.reshape(rc, HEAD)
            h_s[h] = bv.reshape(rc, HEAD)

        project(extra[nh::nh + 1])

        def step(i, cs):
            ii = (nv - 1 - i) if reverse else i
            sl = pl.ds(pl.multiple_of(ii * 8, 8), 8)
            out = []
            for h in range(nh):
                hv = h_s[h, sl, :] + a_s[h, sl, :] * cs[h]
                h_s[h, sl, :] = hv
                out.append(hv[0:1, :] if reverse else hv[7:8, :])
            return tuple(out)

        cs = tuple(carry_ref[:, h * HEAD:(h + 1) * HEAD] for h in range(nh))
        cs = lax.fori_loop(0, nv, step, cs, unroll=True)
        for h in range(nh):
            hs = slice(h * HEAD, (h + 1) * HEAD)
            carry_ref[:, hs] = cs[h]
            if reverse:
                gate = gr_ref[rows, hs].astype(F32)
                o_ref[rows, hs] = ((hf_ref[rows, hs] + h_s[h]) * _silu(gate)).astype(o_ref.dtype)
            else:
                o_ref[rows, hs] = h_s[h]

        return c

    lax.fori_loop(0, nchunk, chunk, 0, unroll=unroll)


def _rglru(proj, hf, hn, w_in, extra_cols, conv_w, conv_b, w_cat, ba, bx, lam, layer, n_a, reverse,
           tb, cb, rc, unroll):
    t = proj.shape[0]
    n_t = t // tb
    ncg = D_MODEL // cb
    hb = tb // HALO
    nhalo = t // HALO
    d = 1 if reverse else 0
    n_extra = len(extra_cols)

    def tmap(ti):
        return (n_t - 1 - ti) if reverse else ti

    blk = lambda col: pl.BlockSpec((tb, cb), lambda c, ti: (tmap(ti), col * ncg + c))
    vec = lambda: pl.BlockSpec((None, 1, cb), lambda c, ti: (layer, 0, c))
    dvec = lambda: pl.BlockSpec((None, None, 1, cb), lambda c, ti: (layer, d, 0, c))
    in_specs = [
        blk(COL_XR),
        pl.BlockSpec((HALO, cb), lambda c, ti: (jnp.maximum(tmap(ti) * hb - 1, 0), c)),
        pl.BlockSpec((HALO, cb), lambda c, ti: (jnp.minimum((tmap(ti) + 1) * hb, nhalo - 1), c)),
    ]
    args = [proj, proj, proj]
    if reverse:
        in_specs += [blk(COL_GR), pl.BlockSpec((tb, cb), lambda c, ti: (tmap(ti), c))]
        args += [proj, hf]
    in_specs += [
        pl.BlockSpec((None, 4, cb), lambda c, ti: (layer, 0, c)),
        vec(),
        pl.BlockSpec((None, None, cb // HEAD, HEAD, 2 * HEAD), lambda c, ti: (layer, d, c, 0, 0)),
        dvec(), dvec(), dvec(),
        pl.BlockSpec((tb, D_MODEL), lambda c, ti: (tmap(ti), 0)),
    ]
    args += [conv_w, conv_b, w_cat, ba, bx, lam, hn]
    for col in extra_cols:
        in_specs.append(pl.BlockSpec((None, D_MODEL, cb), lambda c, ti, col=col: (layer, 0, col * ncg + c),
                                     pipeline_mode=pl.Buffered(1)))
        args.append(w_in)
    out_blk = lambda: pl.BlockSpec((tb, cb), lambda c, ti: (tmap(ti), c))
    nh = cb // HEAD
    return pl.pallas_call(
        functools.partial(_rglru_body, reverse=reverse, n_extra=n_extra, n_a=n_a, n_t=n_t,
                          tb=tb, cb=cb, rc=rc, unroll=unroll),
        grid=(ncg, n_t),
        in_specs=in_specs,
        out_specs=[out_blk() for _ in range(1 + n_extra)],
        out_shape=[jax.ShapeDtypeStruct((t, D_MODEL), BF16 if reverse else F32)]
        + [jax.ShapeDtypeStruct((t, D_MODEL), BF16) for _ in range(n_extra)],
        scratch_shapes=[
            pltpu.VMEM((tb + 16, cb), F32),
            pltpu.VMEM((nh, rc, HEAD), F32),
            pltpu.VMEM((nh, rc, HEAD), F32),
            pltpu.VMEM((1, cb), F32),
            pltpu.VMEM((n_extra, D_MODEL, cb), BF16),
        ],
        compiler_params=pltpu.CompilerParams(
            dimension_semantics=("arbitrary", "arbitrary"), vmem_limit_bytes=VMEM_LIMIT),
        name="rglru_bwd" if reverse else "rglru_fwd",
    )(*args)


def _decay_sum_matrix(reverse):
    t = np.arange(CHUNK)[:, None]
    u = np.arange(CHUNK)[None, :]
    blocks = [u >= t] if reverse else [u <= t]
    for m in LEVELS[1:]:
        upper = (t & m) != 0
        if reverse:
            rho = (t // (2 * m)) * (2 * m) + m
            blocks.append(np.where(upper, (u >= rho) & (u < t), (u >= t) & (u < rho)))
        else:
            rho = (t // (2 * m)) * (2 * m) + m - 1
            blocks.append(np.where(upper, (u > rho) & (u <= t), (u > t) & (u <= rho)))
    mat = np.concatenate(blocks, axis=0).astype(np.float32)
    return np.concatenate([mat, mat, mat], axis=1)


def _hgrn_body(*refs, reverse, layer, n_a, n_t, tb, cb, group):
    if reverse:
        (q_ref, z_ref, v_ref, g_ref, of_ref, lower_ref, gain_ref, m_ref, o_ref, st_ref) = refs
    else:
        (q_ref, z_ref, v_ref, lower_ref, m_ref, o_ref, st_ref) = refs
    nh = cb // HEAD
    nchunk = tb // CHUNK
    t = pl.program_id(1)
    tix = (n_t - 1 - t) if reverse else t
    if reverse:
        start = jnp.logical_or(tix == n_a - 1, tix == n_t - 1)
    else:
        start = jnp.logical_or(tix == 0, tix == n_a)

    @pl.when(start)
    def _():
        st_ref[...] = jnp.zeros_like(st_ref)

    low = lower_ref[...]
    e = jnp.exp(low - jnp.max(low, axis=0, keepdims=True))
    den = jnp.sum(e, axis=0, keepdims=True)
    if layer == 0:
        lb = jnp.zeros_like(den)
    else:
        lb = jnp.sum(e[1:layer + 1, :], axis=0, keepdims=True) / den
    log_lb = jnp.log(lb)
    log1m_lb = jnp.log1p(-lb)
    half_1m_lb = 0.5 * (1.0 - lb)

    ti = lax.broadcasted_iota(jnp.int32, (CHUNK, CHUNK), 0)
    si = lax.broadcasted_iota(jnp.int32, (CHUNK, CHUNK), 1)
    txs = ti ^ si
    causal = (ti <= si) if reverse else (ti >= si)
    rowi = lax.broadcasted_iota(jnp.int32, (CHUNK, HEAD), 0)
    q_side = [((rowi & m) == 0) if reverse else ((rowi & m) != 0) for m in LEVELS]

    hsl = [slice(h * HEAD, (h + 1) * HEAD) for h in range(nh)]
    items = [(j, h) for j in range(group) for h in range(nh)]

    def trip_rows(k):
        rows = []
        for j in range(group):
            kc = k * group + j
            kk = (nchunk - 1 - kc) if reverse else kc
            rows.append(pl.ds(kk * CHUNK, CHUNK))
        return rows

    def gates(rows):
        qk = {}
        rhs = {}
        for j in range(group):
            for pair in range(nh // 2):
                parts = []
                for h in (2 * pair, 2 * pair + 1):
                    hs = hsl[h]
                    z = z_ref[rows[j], hs].astype(F32)
                    q = _silu(q_ref[rows[j], hs].astype(F32))
                    ez = jnp.exp(-jnp.abs(z))
                    log_sig = jnp.minimum(z, 0.0) - jnp.log(1.0 + ez)
                    if layer == 0:
                        logf = log_sig
                    else:
                        u = log1m_lb[:, hs] + log_sig
                        la = log_lb[:, hs]
                        logf = jnp.maximum(la, u) + jnp.log(1.0 + jnp.exp(-jnp.abs(la - u)))
                    key = half_1m_lb[:, hs] - half_1m_lb[:, hs] * jnp.tanh(0.5 * z)
                    logf2 = logf * LOG2E
                    qk[(j, h)] = (q, key, logf2)
                    parts.append(_split3(logf2))
                rhs[(j, pair)] = jnp.concatenate(
                    [jnp.concatenate([parts[0][i], parts[1][i]], axis=1) for i in range(3)], axis=0)
        return qk, rhs

    def decay_sums(rhs):
        dall = {}
        for (j, pair), r in rhs.items():
            dpair = jnp.dot(m_ref[...], r, preferred_element_type=F32)
            dall[(j, 2 * pair)] = dpair[:, :HEAD]
            dall[(j, 2 * pair + 1)] = dpair[:, HEAD:]
        return dall

    def score_free(rows, qk, dall):
        levels = {}
        o_inter = {}
        for j, h in items:
            q, key, logf2 = qk[(j, h)]
            d = dall[(j, h)]
            q16 = q.astype(BF16)
            k16 = key.astype(BF16)
            p = [lax.dot_general(q16, k16, _NT, preferred_element_type=F32)]
            for li in range(len(LEVELS)):
                if li == 0:
                    d_l = jnp.where(q_side[0], logf2, 0.0)
                else:
                    d_l = d[li * CHUNK:(li + 1) * CHUNK, :]
                w_l = jnp.where(q_side[li], q16, k16) * jnp.exp2(d_l.astype(BF16))
                p.append(lax.dot_general(w_l, w_l, _NT, preferred_element_type=F32))
            levels[(j, h)] = p

            b2 = d[0:CHUNK, :]
            st = st_ref[h]
            o_inter[(j, h)] = ((q * jnp.exp2(b2)).astype(BF16), st.astype(BF16))
            total = b2[0:1, :] if reverse else b2[CHUNK - 1:CHUNK, :]
            kd = (key * jnp.exp2(total - b2)).astype(BF16)
            dec = jnp.transpose(jnp.broadcast_to(jnp.exp2(total), (HEAD, HEAD)))
            st_ref[h] = st * dec + lax.dot_general(
                kd, v_ref[rows[j], hsl[h]], _TN, preferred_element_type=F32)
        return levels, o_inter

    def finish(rows, levels, o_inter):
        for j, h in items:
            hs = hsl[h]
            scores = levels[(j, h)][0].astype(BF16)
            for li, m in enumerate(LEVELS):
                scores = jnp.where(txs >= m, levels[(j, h)][li + 1].astype(BF16), scores)
            scores = jnp.where(causal, scores, jnp.zeros_like(scores))
            q_in, st16 = o_inter[(j, h)]
            o = jnp.dot(jnp.concatenate([q_in, scores], axis=1),
                        jnp.concatenate([st16, v_ref[rows[j], hs]], axis=0), preferred_element_type=F32)
            if reverse:
                ot = of_ref[rows[j], hs] + o
                ms = jnp.mean(ot * ot, axis=-1, keepdims=True)
                y = ot * lax.rsqrt(ms + EPS) * gain_ref[:, hs]
                o_ref[rows[j], hs] = (y * _silu(g_ref[rows[j], hs].astype(F32))).astype(o_ref.dtype)
            else:
                o_ref[rows[j], hs] = o

    n_trips = nchunk // group
    rows = trip_rows(0)
    qk, rhs = gates(rows)
    dall = decay_sums(rhs)
    for k in range(n_trips):
        levels, o_inter = score_free(rows, qk, dall)
        if k + 1 < n_trips:
            next_rows = trip_rows(k + 1)
            qk, rhs = gates(next_rows)
            dall = decay_sums(rhs)
        finish(rows, levels, o_inter)
        if k + 1 < n_trips:
            rows = next_rows


def _hgrn(q, z, v, gate, of, hg_lower, hg_gain, layer, n_a, reverse, tb, cb, group):
    t = q.shape[0]
    n_t = t // tb
    depth = hg_lower.shape[1]
    d = 1 if reverse else 0
    msum = jnp.asarray(_decay_sum_matrix(reverse), dtype=BF16)

    def tmap(ti):
        return (n_t - 1 - ti) if reverse else ti

    blk = lambda: pl.BlockSpec((tb, cb), lambda c, ti: (tmap(ti), c))
    in_specs = [blk(), blk(), blk()]
    args = [q, z, v]
    if reverse:
        in_specs += [blk(), blk()]
        args += [gate, of]
    in_specs.append(pl.BlockSpec((None, depth, cb), lambda c, ti: (d, 0, c)))
    args.append(hg_lower)
    if reverse:
        in_specs.append(pl.BlockSpec((None, 1, cb), lambda c, ti: (layer, 0, c)))
        args.append(hg_gain)
    in_specs.append(pl.BlockSpec(msum.shape, lambda c, ti: (0, 0)))
    args.append(msum)
    return pl.pallas_call(
        functools.partial(_hgrn_body, reverse=reverse, layer=layer, n_a=n_a, n_t=n_t, tb=tb, cb=cb,
                          group=group),
        grid=(D_MODEL // cb, n_t),
        in_specs=in_specs,
        out_specs=blk(),
        out_shape=jax.ShapeDtypeStruct((t, D_MODEL), BF16 if reverse else F32),
        scratch_shapes=[pltpu.VMEM((cb // HEAD, HEAD, HEAD), F32)],
        compiler_params=pltpu.CompilerParams(
            dimension_semantics=("arbitrary", "arbitrary"), vmem_limit_bytes=VMEM_LIMIT),
        name="hgrn_bwd" if reverse else "hgrn_fwd",
    )(*args)


def _merge_body(*refs, final, off, n_first):
    x_refs = refs[:-9]
    ar_ref, ah_ref, mr_ref, mh_ref, wr_ref, wh_ref, wo_ref, fg_ref, o_ref = refs[-9:]
    if len(x_refs) == 1:
        x = x_refs[0][...]
    else:
        x = jnp.where(pl.program_id(0) + off < n_first, x_refs[0][...], x_refs[1][...])
    yr = jnp.dot(ar_ref[...], wr_ref[...], preferred_element_type=F32)
    yh = jnp.dot(ah_ref[...], wh_ref[...], preferred_element_type=F32)
    merged = _sigmoid(mr_ref[...].astype(F32)) * yr + _sigmoid(mh_ref[...].astype(F32)) * yh
    x = x + jnp.dot(merged.astype(BF16), wo_ref[...], preferred_element_type=F32)
    if final:
        ms = jnp.mean(x * x, axis=-1, keepdims=True)
        x = x * lax.rsqrt(ms + EPS) * fg_ref[...]
    o_ref[...] = x


def _merge(x, a_r, a_h, m_r, m_h, w_dr, w_dh, w_o, final_gain, layer, final, tm, row0, n_rows):
    off = row0 // tm
    x_specs, x_args = _row_specs(x, tm, off)
    n_first = x_args[0].shape[0] // tm
    row = lambda: pl.BlockSpec((tm, D_MODEL), lambda i: (i + off, 0))
    wgt = lambda: pl.BlockSpec((None, D_MODEL, D_MODEL), lambda i: (layer, 0, 0),
                               pipeline_mode=pl.Buffered(1))
    return pl.pallas_call(
        functools.partial(_merge_body, final=final, off=off, n_first=n_first),
        grid=(n_rows // tm,),
        in_specs=x_specs + [row(), row(), row(), row(), wgt(), wgt(), wgt(),
                            pl.BlockSpec((1, D_MODEL), lambda i: (0, 0))],
        out_specs=pl.BlockSpec((tm, D_MODEL), lambda i: (i, 0)),
        out_shape=jax.ShapeDtypeStruct((n_rows, D_MODEL), F32),
        compiler_params=pltpu.CompilerParams(
            dimension_semantics=("arbitrary",), vmem_limit_bytes=VMEM_LIMIT),
        name="merge",
    )(*x_args, a_r, a_h, m_r, m_h, w_dr, w_dh, w_o, final_gain)


def kernel(x_prompt, x_sample, norm_gain, w_in, conv_w, conv_b, rg_wa, rg_ba, rg_wx, rg_bx, rg_lambda,
           hg_lower, hg_norm_gain, w_down_r, w_down_h, w_out, final_gain):
    depth = w_in.shape[0]
    s_a = x_prompt.shape[0] * x_prompt.shape[1]
    s_b = x_sample.shape[0] * x_sample.shape[1]
    assert x_prompt.shape[0] == 1 and x_sample.shape[0] == 1
    t = s_a + s_b
    x = (x_prompt.reshape(s_a, D_MODEL), x_sample.reshape(s_b, D_MODEL))

    s_min = min(s_a, s_b)
    tb = _largest_tile(s_min, TIME_BLOCK)
    n_a = s_a // tb
    tm_in = _largest_tile(s_min, INPROJ_ROWS_SPLIT_INPUT)
    tm_merge = _largest_tile(s_min, MERGE_ROWS)
    rg_unroll = tb // RG_ROWS

    w_rg_b = w_in[:, :, :2 * D_MODEL].astype(BF16)
    w_dr_b = w_down_r.astype(BF16)
    w_dh_b = w_down_h.astype(BF16)
    w_o_b = w_out.astype(BF16)
    w_cat = jnp.concatenate([rg_wa, rg_wx], axis=-1).astype(BF16)
    gain3 = norm_gain.reshape(depth, 1, D_MODEL)
    conv_b3 = conv_b.reshape(depth, 1, D_MODEL)
    ba4 = rg_ba.reshape(depth, 2, 1, D_MODEL)
    bx4 = rg_bx.reshape(depth, 2, 1, D_MODEL)
    lam4 = rg_lambda.reshape(depth, 2, 1, D_MODEL)
    hgain3 = hg_norm_gain.reshape(depth, 1, D_MODEL)
    fgain = final_gain.reshape(1, D_MODEL)

    for l in range(depth):
        proj, hn = _inproj(x, t, gain3, w_rg_b, l, tm_in if l == 0 else _largest_tile(t, INPROJ_ROWS),
                           INPROJ_COLS)
        rg = (conv_w, conv_b3, w_cat, ba4, bx4, lam4, l, n_a)
        hf, q, zf, v = _rglru(proj, None, hn, w_in, (COL_Q, COL_ZF, COL_V), *rg, False,
                              tb, RG_FWD_CHANNELS, RG_ROWS, rg_unroll)
        a_r, zb, gh, m_r, m_h = _rglru(proj, hf, hn, w_in, (COL_ZB, COL_GH, COL_MR, COL_MH), *rg, True,
                                       tb, RG_BWD_CHANNELS, RG_ROWS, rg_unroll)
        of = _hgrn(q, zf, v, None, None, hg_lower, hgain3, l, n_a, False, tb, HG_CHANNELS, HG_GROUP)
        a_h = _hgrn(q, zb, v, gh, of, hg_lower, hgain3, l, n_a, True, tb, HG_CHANNELS, HG_GROUP)
        mg = (a_r, a_h, m_r, m_h, w_dr_b, w_dh_b, w_o_b, fgain, l)
        if l < depth - 1:
            x = _merge(x, *mg, False, tm_merge, 0, t)
        else:
            y_a = _merge(x, *mg, True, tm_merge, 0, s_a)
            y_b = _merge(x, *mg, True, tm_merge, s_a, s_b)
    return (y_a.reshape(x_prompt.shape), y_b.reshape(x_sample.shape))
```

```python
import functools

import numpy as np
import jax
import jax.numpy as jnp
from jax import lax
from jax.experimental import pallas as pl
from jax.experimental.pallas import tpu as pltpu

F32 = jnp.float32
BF16 = jnp.bfloat16

D_MODEL = 2048
HEAD = 128
N_SPLITS = 9
COL_XR, COL_GR, COL_Q, COL_ZF, COL_ZB, COL_V, COL_GH, COL_MR, COL_MH = range(N_SPLITS)
EPS = 1e-6
RG_C = 8.0
CHUNK = 64
LEVELS = (1, 2, 4, 8, 16, 32)
LOG2E = 1.4426950408889634
HALO = 16
VMEM_LIMIT = 56 * 1024 * 1024

TIME_BLOCK = 1024
RG_ROWS = 128
RG_FWD_CHANNELS = 256
RG_BWD_CHANNELS = 512
HG_CHANNELS = 512
HG_GROUP = 2
DIRECT_GATE_MIN_LB = 1e-3
INPROJ_ROWS = 1024
INPROJ_ROWS_SPLIT_INPUT = 512
INPROJ_COLS = 1024
MERGE_ROWS = 256

_NT = (((1,), (1,)), ((), ()))
_TN = (((0,), (0,)), ((), ()))


def _sigmoid(x):
    return 0.5 * jnp.tanh(0.5 * x) + 0.5


def _silu(x):
    hx = 0.5 * x
    return hx * (1.0 + jnp.tanh(hx))


def _softplus(x):
    return jnp.maximum(x, 0.0) + jnp.log1p(jnp.exp(-jnp.abs(x)))


def _split3(x):
    hi = x.astype(BF16)
    r1 = x - hi.astype(F32)
    mid = r1.astype(BF16)
    lo = (r1 - mid.astype(F32)).astype(BF16)
    return hi, mid, lo


def _largest_tile(n, cap):
    t = cap
    while n % t:
        t //= 2
    return t


def _row_specs(x, tm, off=0):
    if not isinstance(x, tuple):
        return [pl.BlockSpec((tm, D_MODEL), lambda i, *_: (i + off, 0))], [x]
    n_first = x[0].shape[0] // tm
    return [pl.BlockSpec((tm, D_MODEL), lambda i, *_: (jnp.minimum(i + off, n_first - 1), 0)),
            pl.BlockSpec((tm, D_MODEL), lambda i, *_: (jnp.maximum(i + off - n_first, 0), 0))], list(x)


def _inproj_body(*refs, tm, n_first):
    x_refs = refs[:-4]
    g_ref, w_ref, o_ref, h_ref = refs[-4:]
    i = pl.program_id(0)

    def normalise(x_ref):
        rows = 128
        def norm(r, c):
            sl = pl.ds(pl.multiple_of(r * rows, rows), rows)
            x = x_ref[sl, :]
            ms = jnp.mean(x * x, axis=-1, keepdims=True)
            h_ref[sl, :] = (x * lax.rsqrt(ms + EPS) * g_ref[...]).astype(BF16)
            return c
        lax.fori_loop(0, tm // rows, norm, 0)

    first_col = pl.program_id(1) == 0
    if len(x_refs) == 1:
        pl.when(first_col)(lambda: normalise(x_refs[0]))
    else:
        pl.when(jnp.logical_and(first_col, i < n_first))(lambda: normalise(x_refs[0]))
        pl.when(jnp.logical_and(first_col, i >= n_first))(lambda: normalise(x_refs[1]))

    o_ref[...] = jnp.dot(h_ref[...], w_ref[...], preferred_element_type=F32).astype(o_ref.dtype)


def _inproj(x, t, gain, w_in, layer, tm, tn):
    x_specs, x_args = _row_specs(x, tm)
    n_first = x_args[0].shape[0] // tm
    return pl.pallas_call(
        functools.partial(_inproj_body, tm=tm, n_first=n_first),
        grid=(t // tm, 2 * D_MODEL // tn),
        in_specs=x_specs + [
            pl.BlockSpec((None, 1, D_MODEL), lambda i, j: (layer, 0, 0)),
            pl.BlockSpec((None, D_MODEL, tn), lambda i, j: (layer, 0, j)),
        ],
        out_specs=[pl.BlockSpec((tm, tn), lambda i, j: (i, j)),
                   pl.BlockSpec((tm, D_MODEL), lambda i, j: (i, 0))],
        out_shape=[jax.ShapeDtypeStruct((t, 2 * D_MODEL), BF16),
                   jax.ShapeDtypeStruct((t, D_MODEL), BF16)],
        compiler_params=pltpu.CompilerParams(
            dimension_semantics=("arbitrary", "arbitrary"), vmem_limit_bytes=VMEM_LIMIT),
        name="inproj",
    )(*x_args, gain, w_in)


def _rglru_body(*refs, reverse, n_extra, n_a, n_t, tb, cb, rc, unroll):
    refs = list(refs)
    x_ref, xp_ref, xn_ref = refs[:3]
    del refs[:3]
    if reverse:
        gr_ref, hf_ref = refs[:2]
        del refs[:2]
    cw_ref, cbias_ref, w_ref, ba_ref, bx_ref, lam_ref, hn_ref = refs[:7]
    wx_refs = refs[7:7 + n_extra]
    o_ref = refs[7 + n_extra]
    px_refs = refs[8 + n_extra:8 + 2 * n_extra]
    xs_ref, a_s, h_s, carry_ref, wb_ref = refs[8 + 2 * n_extra:]

    nh = cb // HEAD
    nchunk = tb // rc
    nv = rc // 8
    t = pl.program_id(1)
    tix = (n_t - 1 - t) if reverse else t
    seq_first = jnp.logical_or(tix == 0, tix == n_a)
    seq_last = jnp.logical_or(tix == n_a - 1, tix == n_t - 1)
    start = seq_last if reverse else seq_first

    @pl.when(start)
    def _():
        carry_ref[...] = jnp.zeros_like(carry_ref)

    @pl.when(t == 0)
    def _():
        for e, wx_ref in enumerate(wx_refs):
            wb_ref[e] = wx_ref[...].astype(BF16)

    pm = jnp.where(seq_first, 0.0, 1.0).astype(F32)
    nm = jnp.where(seq_last, 0.0, 1.0).astype(F32)
    xs_ref[0:8, :] = xp_ref[...].astype(F32)[HALO - 8:HALO, :] * pm
    xs_ref[8:8 + tb, :] = x_ref[...].astype(F32)
    xs_ref[8 + tb:16 + tb, :] = xn_ref[...].astype(F32)[0:8, :] * nm

    neg_c_sp = (-RG_C * LOG2E) * _softplus(-lam_ref[...])
    sub = lax.broadcasted_iota(jnp.int32, (nv, 8, HEAD), 1)

    def chunk(k, c):
        kk = (nchunk - 1 - k) if reverse else k
        r0 = pl.multiple_of(kk * rc, rc)
        rows = pl.ds(r0, rc)
        hn = hn_ref[rows, :]
        extra = list(enumerate(px_refs))

        def project(pairs):
            for e, px_ref in pairs:
                px_ref[rows, :] = jnp.dot(hn, wb_ref[e], preferred_element_type=F32).astype(px_ref.dtype)

        for h in range(nh):
            project(extra[h::nh + 1])
            hs = slice(h * HEAD, (h + 1) * HEAD)
            xw = xs_ref[pl.ds(r0, rc + 16), hs].reshape(nv + 2, 8, HEAD)

            def tap(s):
                if s == 0:
                    return xw[1:nv + 1]
                rt = pltpu.roll(xw, s % 8, axis=1)
                if s > 0:
                    return jnp.where(sub < s, rt[0:nv], rt[1:nv + 1])
                return jnp.where(sub < 8 + s, rt[1:nv + 1], rt[2:nv + 2])

            xc = (cbias_ref[:, hs] + tap(2) * cw_ref[0:1, hs] + tap(1) * cw_ref[1:2, hs]
                  + tap(0) * cw_ref[2:3, hs] + tap(-1) * cw_ref[3:4, hs]).reshape(rc, HEAD)
            g = jnp.dot(xc.astype(BF16), w_ref[h], preferred_element_type=F32)
            r = _sigmoid(g[:, :HEAD] + ba_ref[:, hs])
            i = _sigmoid(g[:, HEAD:] + bx_ref[:, hs])
            a = jnp.exp2(r * neg_c_sp[:, hs])
            bv = jnp.sqrt(1.0 - a * a) * (i * xc)
            a = a.reshape(nv, 8, HEAD)
            bv = bv.reshape(nv, 8, HEAD)
            for d in (1, 2, 4):
                a_sh = pltpu.roll(a, (8 - d) if reverse else d, axis=1)
                b_sh = pltpu.roll(bv, (8 - d) if reverse else d, axis=1)
                valid = (sub < 8 - d) if reverse else (sub >= d)
                bv = jnp.where(valid, a * b_sh + bv, bv)
                a = jnp.where(valid, a * a_sh, a)
            a_s[h] = a.reshape(rc, HEAD)
            h_s[h] = bv.reshape(rc, HEAD)

        project(extra[nh::nh + 1])

        def step(i, cs):
            ii = (nv - 1 - i) if reverse else i
            sl = pl.ds(pl.multiple_of(ii * 8, 8), 8)
            out = []
            for h in range(nh):
                hv = h_s[h, sl, :] + a_s[h, sl, :] * cs[h]
                h_s[h, sl, :] = hv
                out.append(hv[0:1, :] if reverse else hv[7:8, :])
            return tuple(out)

        cs = tuple(carry_ref[:, h * HEAD:(h + 1) * HEAD] for h in range(nh))
        cs = lax.fori_loop(0, nv, step, cs, unroll=True)
        for h in range(nh):
            hs = slice(h * HEAD, (h + 1) * HEAD)
            carry_ref[:, hs] = cs[h]
            if reverse:
                gate = gr_ref[rows, hs].astype(F32)
                o_ref[rows, hs] = ((hf_ref[rows, hs] + h_s[h]) * _silu(gate)).astype(o_ref.dtype)
            else:
                o_ref[rows, hs] = h_s[h]

        return c

    lax.fori_loop(0, nchunk, chunk, 0, unroll=unroll)


def _rglru(proj, hf, hn, w_in, extra_cols, conv_w, conv_b, w_cat, ba, bx, lam, layer, n_a, reverse,
           tb, cb, rc, unroll):
    t = proj.shape[0]
    n_t = t // tb
    ncg = D_MODEL // cb
    hb = tb // HALO
    nhalo = t // HALO
    d = 1 if reverse else 0
    n_extra = len(extra_cols)

    def tmap(ti):
        return (n_t - 1 - ti) if reverse else ti

    blk = lambda col: pl.BlockSpec((tb, cb), lambda c, ti: (tmap(ti), col * ncg + c))
    vec = lambda: pl.BlockSpec((None, 1, cb), lambda c, ti: (layer, 0, c))
    dvec = lambda: pl.BlockSpec((None, None, 1, cb), lambda c, ti: (layer, d, 0, c))
    in_specs = [
        blk(COL_XR),
        pl.BlockSpec((HALO, cb), lambda c, ti: (jnp.maximum(tmap(ti) * hb - 1, 0), c)),
        pl.BlockSpec((HALO, cb), lambda c, ti: (jnp.minimum((tmap(ti) + 1) * hb, nhalo - 1), c)),
    ]
    args = [proj, proj, proj]
    if reverse:
        in_specs += [blk(COL_GR), pl.BlockSpec((tb, cb), lambda c, ti: (tmap(ti), c))]
        args += [proj, hf]
    in_specs += [
        pl.BlockSpec((None, 4, cb), lambda c, ti: (layer, 0, c)),
        vec(),
        pl.BlockSpec((None, None, cb // HEAD, HEAD, 2 * HEAD), lambda c, ti: (layer, d, c, 0, 0)),
        dvec(), dvec(), dvec(),
        pl.BlockSpec((tb, D_MODEL), lambda c, ti: (tmap(ti), 0)),
    ]
    args += [conv_w, conv_b, w_cat, ba, bx, lam, hn]
    for col in extra_cols:
        in_specs.append(pl.BlockSpec((None, D_MODEL, cb), lambda c, ti, col=col: (layer, 0, col * ncg + c),
                                     pipeline_mode=pl.Buffered(1)))
        args.append(w_in)
    out_blk = lambda: pl.BlockSpec((tb, cb), lambda c, ti: (tmap(ti), c))
    nh = cb // HEAD
    return pl.pallas_call(
        functools.partial(_rglru_body, reverse=reverse, n_extra=n_extra, n_a=n_a, n_t=n_t,
                          tb=tb, cb=cb, rc=rc, unroll=unroll),
        grid=(ncg, n_t),
        in_specs=in_specs,
        out_specs=[out_blk() for _ in range(1 + n_extra)],
        out_shape=[jax.ShapeDtypeStruct((t, D_MODEL), BF16 if reverse else F32)]
        + [jax.ShapeDtypeStruct((t, D_MODEL), BF16) for _ in range(n_extra)],
        scratch_shapes=[
            pltpu.VMEM((tb + 16, cb), F32),
            pltpu.VMEM((nh, rc, HEAD), F32),
            pltpu.VMEM((nh, rc, HEAD), F32),
            pltpu.VMEM((1, cb), F32),
            pltpu.VMEM((n_extra, D_MODEL, cb), BF16),
        ],
        compiler_params=pltpu.CompilerParams(
            dimension_semantics=("arbitrary", "arbitrary"), vmem_limit_bytes=VMEM_LIMIT),
        name="rglru_bwd" if reverse else "rglru_fwd",
    )(*args)


def _decay_sum_matrix(reverse):
    t = np.arange(CHUNK)[:, None]
    u = np.arange(CHUNK)[None, :]
    blocks = [u >= t] if reverse else [u <= t]
    for m in LEVELS[1:]:
        upper = (t & m) != 0
        if reverse:
            rho = (t // (2 * m)) * (2 * m) + m
            blocks.append(np.where(upper, (u >= rho) & (u < t), (u >= t) & (u < rho)))
        else:
            rho = (t // (2 * m)) * (2 * m) + m - 1
            blocks.append(np.where(upper, (u > rho) & (u <= t), (u > t) & (u <= rho)))
    mat = np.concatenate(blocks, axis=0).astype(np.float32)
    return np.concatenate([mat, mat, mat], axis=1)


def _hgrn_body(*refs, reverse, layer, direct_gate, n_a, n_t, tb, cb, group):
    if reverse:
        (q_ref, z_ref, v_ref, g_ref, of_ref, lower_ref, gain_ref, m_ref, o_ref, st_ref) = refs
    else:
        (q_ref, z_ref, v_ref, lower_ref, m_ref, o_ref, st_ref) = refs
    nh = cb // HEAD
    nchunk = tb // CHUNK
    t = pl.program_id(1)
    tix = (n_t - 1 - t) if reverse else t
    if reverse:
        start = jnp.logical_or(tix == n_a - 1, tix == n_t - 1)
    else:
        start = jnp.logical_or(tix == 0, tix == n_a)

    @pl.when(start)
    def _():
        st_ref[...] = jnp.zeros_like(st_ref)

    low = lower_ref[...]
    e = jnp.exp(low - jnp.max(low, axis=0, keepdims=True))
    den = jnp.sum(e, axis=0, keepdims=True)
    if layer == 0:
        lb = jnp.zeros_like(den)
    else:
        lb = jnp.sum(e[1:layer + 1, :], axis=0, keepdims=True) / den
    log_lb = jnp.log(lb)
    log1m_lb = jnp.log1p(-lb)
    half_1m_lb = 0.5 * (1.0 - lb)
    half_1p_lb = 0.5 * (1.0 + lb)

    ti = lax.broadcasted_iota(jnp.int32, (CHUNK, CHUNK), 0)
    si = lax.broadcasted_iota(jnp.int32, (CHUNK, CHUNK), 1)
    txs = ti ^ si
    causal = (ti <= si) if reverse else (ti >= si)
    rowi = lax.broadcasted_iota(jnp.int32, (CHUNK, HEAD), 0)
    q_side = [((rowi & m) == 0) if reverse else ((rowi & m) != 0) for m in LEVELS]

    hsl = [slice(h * HEAD, (h + 1) * HEAD) for h in range(nh)]
    items = [(j, h) for j in range(group) for h in range(nh)]

    def trip_rows(k):
        rows = []
        for j in range(group):
            kc = k * group + j
            kk = (nchunk - 1 - kc) if reverse else kc
            rows.append(pl.ds(kk * CHUNK, CHUNK))
        return rows

    def gates(rows):
        qk = {}
        rhs = {}
        for j in range(group):
            for pair in range(nh // 2):
                parts = []
                for h in (2 * pair, 2 * pair + 1):
                    hs = hsl[h]
                    z = z_ref[rows[j], hs].astype(F32)
                    q = _silu(q_ref[rows[j], hs].astype(F32))
                    if direct_gate:
                        f = jnp.maximum(half_1p_lb[:, hs] + half_1m_lb[:, hs] * jnp.tanh(0.5 * z),
                                        lb[:, hs])
                        logf2 = jnp.log2(f)
                        qk[(j, h)] = (q, 1.0 - f, logf2)
                        parts.append(_split3(logf2))
                        continue
                    ez = jnp.exp(-jnp.abs(z))
                    log_sig = jnp.minimum(z, 0.0) - jnp.log(1.0 + ez)
                    if layer == 0:
                        logf = log_sig
                    else:
                        u = log1m_lb[:, hs] + log_sig
                        la = log_lb[:, hs]
                        logf = jnp.maximum(la, u) + jnp.log(1.0 + jnp.exp(-jnp.abs(la - u)))
                    key = half_1m_lb[:, hs] - half_1m_lb[:, hs] * jnp.tanh(0.5 * z)
                    logf2 = logf * LOG2E
                    qk[(j, h)] = (q, key, logf2)
                    parts.append(_split3(logf2))
                rhs[(j, pair)] = jnp.concatenate(
                    [jnp.concatenate([parts[0][i], parts[1][i]], axis=1) for i in range(3)], axis=0)
        return qk, rhs

    def decay_sums(rhs):
        dall = {}
        for (j, pair), r in rhs.items():
            dpair = jnp.dot(m_ref[...], r, preferred_element_type=F32)
            dall[(j, 2 * pair)] = dpair[:, :HEAD]
            dall[(j, 2 * pair + 1)] = dpair[:, HEAD:]
        return dall

    def score_free(rows, qk, dall):
        levels = {}
        o_inter = {}
        for j, h in items:
            q, key, logf2 = qk[(j, h)]
            d = dall[(j, h)]
            q16 = q.astype(BF16)
            k16 = key.astype(BF16)
            p = [lax.dot_general(q16, k16, _NT, preferred_element_type=F32)]
            for li in range(len(LEVELS)):
                if li == 0:
                    d_l = jnp.where(q_side[0], logf2, 0.0)
                else:
                    d_l = d[li * CHUNK:(li + 1) * CHUNK, :]
                w_l = jnp.where(q_side[li], q16, k16) * jnp.exp2(d_l.astype(BF16))
                p.append(lax.dot_general(w_l, w_l, _NT, preferred_element_type=F32))
            levels[(j, h)] = p

            b2 = d[0:CHUNK, :]
            st = st_ref[h]
            o_inter[(j, h)] = ((q * jnp.exp2(b2)).astype(BF16), st.astype(BF16))
            total = b2[0:1, :] if reverse else b2[CHUNK - 1:CHUNK, :]
            kd = (key * jnp.exp2(total - b2)).astype(BF16)
            dec = jnp.transpose(jnp.broadcast_to(jnp.exp2(total), (HEAD, HEAD)))
            st_ref[h] = st * dec + lax.dot_general(
                kd, v_ref[rows[j], hsl[h]], _TN, preferred_element_type=F32)
        return levels, o_inter

    def finish(rows, levels, o_inter):
        for j, h in items:
            hs = hsl[h]
            scores = levels[(j, h)][0].astype(BF16)
            for li, m in enumerate(LEVELS):
                scores = jnp.where(txs >= m, levels[(j, h)][li + 1].astype(BF16), scores)
            scores = jnp.where(causal, scores, jnp.zeros_like(scores))
            q_in, st16 = o_inter[(j, h)]
            o = jnp.dot(jnp.concatenate([q_in, scores], axis=1),
                        jnp.concatenate([st16, v_ref[rows[j], hs]], axis=0), preferred_element_type=F32)
            if reverse:
                ot = of_ref[rows[j], hs] + o
                ms = jnp.mean(ot * ot, axis=-1, keepdims=True)
                y = ot * lax.rsqrt(ms + EPS) * gain_ref[:, hs]
                o_ref[rows[j], hs] = (y * _silu(g_ref[rows[j], hs].astype(F32))).astype(o_ref.dtype)
            else:
                o_ref[rows[j], hs] = o

    n_trips = nchunk // group
    rows = trip_rows(0)
    qk, rhs = gates(rows)
    dall = decay_sums(rhs)
    for k in range(n_trips):
        levels, o_inter = score_free(rows, qk, dall)
        if k + 1 < n_trips:
            next_rows = trip_rows(k + 1)
            qk, rhs = gates(next_rows)
            dall = decay_sums(rhs)
        finish(rows, levels, o_inter)
        if k + 1 < n_trips:
            rows = next_rows


def _hgrn(q, z, v, gate, of, hg_lower, hg_gain, layer, n_a, reverse, tb, cb, group, direct_gate=False):
    t = q.shape[0]
    n_t = t // tb
    depth = hg_lower.shape[1]
    d = 1 if reverse else 0
    msum = jnp.asarray(_decay_sum_matrix(reverse), dtype=BF16)

    def tmap(ti):
        return (n_t - 1 - ti) if reverse else ti

    blk = lambda: pl.BlockSpec((tb, cb), lambda c, ti: (tmap(ti), c))
    in_specs = [blk(), blk(), blk()]
    args = [q, z, v]
    if reverse:
        in_specs += [blk(), blk()]
        args += [gate, of]
    in_specs.append(pl.BlockSpec((None, depth, cb), lambda c, ti: (d, 0, c)))
    args.append(hg_lower)
    if reverse:
        in_specs.append(pl.BlockSpec((None, 1, cb), lambda c, ti: (layer, 0, c)))
        args.append(hg_gain)
    in_specs.append(pl.BlockSpec(msum.shape, lambda c, ti: (0, 0)))
    args.append(msum)
    return pl.pallas_call(
        functools.partial(_hgrn_body, reverse=reverse, layer=layer, direct_gate=direct_gate, n_a=n_a,
                          n_t=n_t, tb=tb, cb=cb, group=group),
        grid=(D_MODEL // cb, n_t),
        in_specs=in_specs,
        out_specs=blk(),
        out_shape=jax.ShapeDtypeStruct((t, D_MODEL), BF16 if reverse else F32),
        scratch_shapes=[pltpu.VMEM((cb // HEAD, HEAD, HEAD), F32)],
        compiler_params=pltpu.CompilerParams(
            dimension_semantics=("arbitrary", "arbitrary"), vmem_limit_bytes=VMEM_LIMIT),
        name="hgrn_bwd" if reverse else "hgrn_fwd",
    )(*args)


def _merge_body(*refs, final, off, n_first):
    x_refs = refs[:-9]
    ar_ref, ah_ref, mr_ref, mh_ref, wr_ref, wh_ref, wo_ref, fg_ref, o_ref = refs[-9:]
    if len(x_refs) == 1:
        x = x_refs[0][...]
    else:
        x = jnp.where(pl.program_id(0) + off < n_first, x_refs[0][...], x_refs[1][...])
    yr = jnp.dot(ar_ref[...], wr_ref[...], preferred_element_type=F32)
    yh = jnp.dot(ah_ref[...], wh_ref[...], preferred_element_type=F32)
    merged = _sigmoid(mr_ref[...].astype(F32)) * yr + _sigmoid(mh_ref[...].astype(F32)) * yh
    x = x + jnp.dot(merged.astype(BF16), wo_ref[...], preferred_element_type=F32)
    if final:
        ms = jnp.mean(x * x, axis=-1, keepdims=True)
        x = x * lax.rsqrt(ms + EPS) * fg_ref[...]
    o_ref[...] = x


def _merge(x, a_r, a_h, m_r, m_h, w_dr, w_dh, w_o, final_gain, layer, final, tm, row0, n_rows):
    off = row0 // tm
    x_specs, x_args = _row_specs(x, tm, off)
    n_first = x_args[0].shape[0] // tm
    row = lambda: pl.BlockSpec((tm, D_MODEL), lambda i: (i + off, 0))
    wgt = lambda: pl.BlockSpec((None, D_MODEL, D_MODEL), lambda i: (layer, 0, 0),
                               pipeline_mode=pl.Buffered(1))
    return pl.pallas_call(
        functools.partial(_merge_body, final=final, off=off, n_first=n_first),
        grid=(n_rows // tm,),
        in_specs=x_specs + [row(), row(), row(), row(), wgt(), wgt(), wgt(),
                            pl.BlockSpec((1, D_MODEL), lambda i: (0, 0))],
        out_specs=pl.BlockSpec((tm, D_MODEL), lambda i: (i, 0)),
        out_shape=jax.ShapeDtypeStruct((n_rows, D_MODEL), F32),
        compiler_params=pltpu.CompilerParams(
            dimension_semantics=("arbitrary",), vmem_limit_bytes=VMEM_LIMIT),
        name="merge",
    )(*x_args, a_r, a_h, m_r, m_h, w_dr, w_dh, w_o, final_gain)


def kernel(x_prompt, x_sample, norm_gain, w_in, conv_w, conv_b, rg_wa, rg_ba, rg_wx, rg_bx, rg_lambda,
           hg_lower, hg_norm_gain, w_down_r, w_down_h, w_out, final_gain):
    depth = w_in.shape[0]
    s_a = x_prompt.shape[0] * x_prompt.shape[1]
    s_b = x_sample.shape[0] * x_sample.shape[1]
    assert x_prompt.shape[0] == 1 and x_sample.shape[0] == 1
    t = s_a + s_b
    x = (x_prompt.reshape(s_a, D_MODEL), x_sample.reshape(s_b, D_MODEL))

    s_min = min(s_a, s_b)
    tb = _largest_tile(s_min, TIME_BLOCK)
    n_a = s_a // tb
    tm_in = _largest_tile(s_min, INPROJ_ROWS_SPLIT_INPUT)
    tm_merge = _largest_tile(s_min, MERGE_ROWS)
    rg_unroll = tb // RG_ROWS

    lb_all = jnp.cumsum(jax.nn.softmax(hg_lower.astype(F32), axis=1), axis=1)
    lb_min = jnp.min(lb_all - lb_all[:, :1], axis=-1)
    w_rg_b = w_in[:, :, :2 * D_MODEL].astype(BF16)
    w_dr_b = w_down_r.astype(BF16)
    w_dh_b = w_down_h.astype(BF16)
    w_o_b = w_out.astype(BF16)
    w_cat = jnp.concatenate([rg_wa, rg_wx], axis=-1).astype(BF16)
    gain3 = norm_gain.reshape(depth, 1, D_MODEL)
    conv_b3 = conv_b.reshape(depth, 1, D_MODEL)
    ba4 = rg_ba.reshape(depth, 2, 1, D_MODEL)
    bx4 = rg_bx.reshape(depth, 2, 1, D_MODEL)
    lam4 = rg_lambda.reshape(depth, 2, 1, D_MODEL)
    hgain3 = hg_norm_gain.reshape(depth, 1, D_MODEL)
    fgain = final_gain.reshape(1, D_MODEL)

    for l in range(depth):
        proj, hn = _inproj(x, t, gain3, w_rg_b, l, tm_in if l == 0 else _largest_tile(t, INPROJ_ROWS),
                           INPROJ_COLS)
        rg = (conv_w, conv_b3, w_cat, ba4, bx4, lam4, l, n_a)
        hf, q, zf, v = _rglru(proj, None, hn, w_in, (COL_Q, COL_ZF, COL_V), *rg, False,
                              tb, RG_FWD_CHANNELS, RG_ROWS, rg_unroll)
        a_r, zb, gh, m_r, m_h = _rglru(proj, hf, hn, w_in, (COL_ZB, COL_GH, COL_MR, COL_MH), *rg, True,
                                       tb, RG_BWD_CHANNELS, RG_ROWS, rg_unroll)
        def hgrn(zz, gate, prev, reverse, l=l, q=q, v=v):
            run = functools.partial(_hgrn, q, zz, v, gate, prev, hg_lower, hgain3, l, n_a, reverse, tb,
                                    HG_CHANNELS, HG_GROUP)
            if l == 0:
                return run()
            return lax.cond(lb_min[1 if reverse else 0, l] >= DIRECT_GATE_MIN_LB,
                            lambda: run(direct_gate=True), lambda: run(direct_gate=False))

        of = hgrn(zf, None, None, False)
        a_h = hgrn(zb, gh, of, True)
        mg = (a_r, a_h, m_r, m_h, w_dr_b, w_dh_b, w_o_b, fgain, l)
        if l < depth - 1:
            x = _merge(x, *mg, False, tm_merge, 0, t)
        else:
            y_a = _merge(x, *mg, True, tm_merge, 0, s_a)
            y_b = _merge(x, *mg, True, tm_merge, s_a, s_b)
    return (y_a.reshape(x_prompt.shape), y_b.reshape(x_sample.shape))
```
